```python
import math
import jax, jax.numpy as jnp
from jax import lax
import numpy as np

D_MODEL = 4096
BATCH = 4
SEQ = 2048
DEPTH = 1
DEC_BATCH = 32
DEC_SEQ = 8
PAST_LEN = 8192
PAGE_SIZE = 128

D_ATT = D_MODEL // 2
HEAD_DIM = 128
N_ATT_HEADS = D_ATT // HEAD_DIM
DILATED_BRANCHES = ((128, 1), (512, 4), (2048, 16))
MAX_WINDOW = 2048
N_BUCKETS = 32
BUCKET_MAX_DIST = MAX_WINDOW
D_RNN = D_MODEL - D_ATT
RNN_BLOCKS = 16
RNN_BLOCK_W = D_RNN // RNN_BLOCKS
CONV_W = 4
LRU_C = 8.0
D_IN = 3 * D_ATT + 2 * D_RNN
PEER_HEADS = 8
PEER_KEYS = 128
PEER_EXPERTS = PEER_KEYS * PEER_KEYS
PEER_TOPK = 16
PEER_QDIM = 256
PEER_HALF = PEER_QDIM // 2
PEER_BLOCK = 64
RMS_EPS = 1e-6
NEG_INF = -1e30

kernel_name = 'hybrid_dilated_swa_rglru_peer_step'


def _rmsnorm(x, g):
    xf = x.astype(jnp.float32)
    y = xf * lax.rsqrt(jnp.mean(xf * xf, axis=-1, keepdims=True) + RMS_EPS)
    return (y * g.astype(jnp.float32)).astype(x.dtype)


def _t5_bucket(dist):
    max_exact = N_BUCKETS // 2
    d_f = jnp.maximum(dist, max_exact).astype(jnp.float32)
    large = max_exact + (jnp.log(d_f / max_exact) / math.log(BUCKET_MAX_DIST / max_exact)
                         * (N_BUCKETS - max_exact)).astype(jnp.int32)
    large = jnp.minimum(large, N_BUCKETS - 1)
    return jnp.where(dist < max_exact, dist, large)


def _branch_bias(dist, rel_bias):
    return jnp.moveaxis(rel_bias[_t5_bucket(dist)].astype(jnp.float32), -1, 0)


def _window_softmax(q, k, v, bias, mask):
    logits = jnp.einsum('...qhd,...khd->...hqk', q, k,
                        preferred_element_type=jnp.float32) * (HEAD_DIM ** -0.5) + bias
    logits = jnp.where(mask, logits, NEG_INF)
    m = jnp.max(logits, axis=-1, keepdims=True)
    p = jnp.exp(logits - m)
    s = jnp.sum(p, axis=-1)
    o = jnp.einsum('...hqk,...khd->...qhd', p, v.astype(jnp.float32))
    o = o / jnp.swapaxes(s, -1, -2)[..., None]
    lse = m[..., 0] + jnp.log(s)
    return o, lse


def _combine_branches(outs, lses):
    w = jax.nn.softmax(jnp.stack(lses), axis=0)
    return jnp.sum(w[..., None] * jnp.stack(outs), axis=0)


def _dilated_attn_prompt(q, k, v, rel_bias):
    B, S, H, Dh = q.shape
    outs, lses = [], []
    for window, dil in DILATED_BRANCHES:
        n_k = window // dil
        m_len = S // dil
        n_blk = -(-m_len // n_k)
        m_pad = n_blk * n_k

        def strided(t):
            return t.reshape(B, m_len, dil, H, Dh).transpose(0, 2, 1, 3, 4)

        qs = jnp.pad(strided(q), ((0, 0), (0, 0), (0, m_pad - m_len), (0, 0), (0, 0)))
        qs = qs.reshape(B, dil, n_blk, n_k, H, Dh)

        def band(t):
            tp = jnp.pad(strided(t), ((0, 0), (0, 0), (n_k, m_pad - m_len), (0, 0), (0, 0)))
            tp = tp.reshape(B, dil, n_blk + 1, n_k, H, Dh)
            return jnp.concatenate([tp[:, :, :-1], tp[:, :, 1:]], axis=3)

        kb, vb = band(k), band(v)
        qi = jnp.arange(n_k)[:, None]
        ki = jnp.arange(2 * n_k)[None, :]
        delta = qi - ki + n_k
        key_sub = jnp.arange(n_blk)[:, None, None] * n_k + ki[None] - n_k
        mask = (delta >= 0) & (delta <= n_k) & (key_sub >= 0)
        bias = _branch_bias(jnp.clip(delta, 0, n_k) * dil, rel_bias)
        o, lse = _window_softmax(qs, kb, vb, bias, mask[None, None, :, None])
        o = o.reshape(B, dil, m_pad, H, Dh)[:, :, :m_len].transpose(0, 2, 1, 3, 4).reshape(B, S, H, Dh)
        lse = jnp.swapaxes(lse, -1, -2).reshape(B, dil, m_pad, H)[:, :, :m_len]
        lse = lse.transpose(0, 2, 1, 3).reshape(B, S, H)
        outs.append(o)
        lses.append(lse)
    return _combine_branches(outs, lses)


def _dilated_attn_sample(q, k, v, buf_k, buf_v, rel_bias):
    B, Q, H, Dh = q.shape
    buf = buf_k.shape[1]
    k_ctx = jnp.concatenate([buf_k.astype(k.dtype), k], axis=1)
    v_ctx = jnp.concatenate([buf_v.astype(v.dtype), v], axis=1)
    q_local = buf + jnp.arange(Q)
    outs, lses = [], []
    for window, dil in DILATED_BRANCHES:
        n_k = window // dil
        steps = jnp.arange(n_k + 1)
        idx = q_local[:, None] - steps[None, :] * dil
        valid = idx >= 0
        idx = jnp.maximum(idx, 0)
        kg = k_ctx[:, idx]
        vg = v_ctx[:, idx]
        bias = _branch_bias(steps * dil, rel_bias)[:, None, :]
        o, lse = _window_softmax(q[:, :, None], kg, vg, bias, valid[None, :, None, None, :])
        outs.append(o[:, :, 0])
        lses.append(lse[..., 0])
    return _combine_branches(outs, lses), k_ctx[:, -buf:], v_ctx[:, -buf:]


def _rglru_mixer(xr, gate, conv_prev, h_prev, conv_w, conv_b, w_rgate, b_rgate,
                 w_igate, b_igate, lru_lambda):
    B, T, _ = xr.shape
    xc_full = jnp.concatenate([conv_prev.astype(xr.dtype), xr], axis=1)
    xc = conv_b + sum(xc_full[:, j:j + T] * conv_w[j] for j in range(CONV_W))
    new_conv = xc_full[:, -(CONV_W - 1):]
    xb = xc.reshape(B, T, RNN_BLOCKS, RNN_BLOCK_W)
    r = jax.nn.sigmoid(jnp.einsum('btnc,ncd->btnd', xb, w_rgate, preferred_element_type=jnp.float32)
                       + b_rgate.astype(jnp.float32)).reshape(B, T, D_RNN)
    i = jax.nn.sigmoid(jnp.einsum('btnc,ncd->btnd', xb, w_igate, preferred_element_type=jnp.float32)
                       + b_igate.astype(jnp.float32)).reshape(B, T, D_RNN)
    log_a = -LRU_C * r * jax.nn.softplus(-lru_lambda.astype(jnp.float32))
    a = jnp.exp(log_a)
    u = jnp.sqrt(-jnp.expm1(2.0 * log_a)) * i * xc.astype(jnp.float32)

    def step(h, au):
        a_t, u_t = au
        h = a_t * h + u_t
        return h, h

    h_last, hs = lax.scan(step, h_prev.astype(jnp.float32),
                          (jnp.swapaxes(a, 0, 1), jnp.swapaxes(u, 0, 1)))
    hs = jnp.swapaxes(hs, 0, 1)
    y = (hs * jax.nn.gelu(gate.astype(jnp.float32))).astype(xr.dtype)
    return y, new_conv, h_last.astype(xr.dtype)


def _peer(x2d, w_q, keys, u_tab, v_tab):
    T = x2d.shape[0]
    n_blk = -(-T // PEER_BLOCK)
    xp = jnp.pad(x2d, ((0, n_blk * PEER_BLOCK - T), (0, 0))).reshape(n_blk, PEER_BLOCK, D_MODEL)

    def block(xb):
        q = (xb @ w_q).reshape(PEER_BLOCK, PEER_HEADS, 2, PEER_HALF)
        s = jnp.einsum('thpc,hpnc->thpn', q, keys, preferred_element_type=jnp.float32)
        sv, si = lax.top_k(s, PEER_TOPK)
        cand_s = (sv[:, :, 0, :, None] + sv[:, :, 1, None, :]).reshape(PEER_BLOCK, PEER_HEADS, PEER_TOPK * PEER_TOPK)
        cand_id = (si[:, :, 0, :, None] * PEER_KEYS + si[:, :, 1, None, :]).reshape(PEER_BLOCK, PEER_HEADS, PEER_TOPK * PEER_TOPK)
        top_s, pos = lax.top_k(cand_s, PEER_TOPK)
        eid = jnp.take_along_axis(cand_id, pos, axis=-1)
        g = jax.nn.softmax(top_s, axis=-1)
        act = jax.nn.gelu(jnp.einsum('td,thkd->thk', xb, u_tab[eid], preferred_element_type=jnp.float32),
                          approximate=False)
        return jnp.einsum('thk,thkd->td', (g * act).astype(xb.dtype), v_tab[eid])

    return lax.map(block, xp).reshape(n_blk * PEER_BLOCK, D_MODEL)[:T]


def _mixer_inputs(x, norm1_g, w_in, q_norm_g, k_norm_g):
    B, T, _ = x.shape
    proj = _rmsnorm(x, norm1_g) @ w_in
    q, k, v, xr, gate = jnp.split(proj, [D_ATT, 2 * D_ATT, 3 * D_ATT, 3 * D_ATT + D_RNN], axis=-1)
    heads = lambda t: t.reshape(B, T, N_ATT_HEADS, HEAD_DIM)
    return _rmsnorm(heads(q), q_norm_g), _rmsnorm(heads(k), k_norm_g), heads(v), xr, gate


def _layer_out(x, attn, y_rnn, w_out, norm2_g, peer_w_q, peer_keys, peer_u, peer_v):
    B, T, _ = x.shape
    mixed = jnp.concatenate([attn.astype(x.dtype).reshape(B, T, D_ATT), y_rnn], axis=-1) @ w_out
    h = x + mixed
    ffn = _peer(_rmsnorm(h, norm2_g).reshape(B * T, D_MODEL), peer_w_q, peer_keys, peer_u, peer_v)
    return h + ffn.reshape(B, T, D_MODEL)


def setup_inputs(seed: int = 0) -> dict:
    key = jax.random.key(seed)
    ks = jax.random.split(key, 24)
    f32 = jnp.float32
    nrm = lambda k, shape, scale: jax.random.normal(k, shape, f32) * scale
    win_buf = min(MAX_WINDOW, PAST_LEN)
    a_target = jax.random.uniform(ks[17], (DEPTH, D_RNN), f32, 0.9, 0.999)
    base = a_target ** (1.0 / LRU_C)
    return {
        'x_prompt': nrm(ks[0], (BATCH, SEQ, D_MODEL), 1.0),
        'x_sample': nrm(ks[1], (DEC_BATCH, DEC_SEQ, D_MODEL), 1.0),
        'cache_win_k': nrm(ks[2], (DEPTH, DEC_BATCH, win_buf, N_ATT_HEADS, HEAD_DIM), 1.0),
        'cache_win_v': nrm(ks[3], (DEPTH, DEC_BATCH, win_buf, N_ATT_HEADS, HEAD_DIM), 1.0),
        'state_conv': nrm(ks[4], (DEPTH, DEC_BATCH, CONV_W - 1, D_RNN), 1.0),
        'state_rglru': nrm(ks[5], (DEPTH, DEC_BATCH, D_RNN), 0.5),
        'rel_bias': nrm(ks[6], (N_BUCKETS, N_ATT_HEADS), 0.5),
        'norm1_g': 1.0 + nrm(ks[7], (DEPTH, D_MODEL), 0.05),
        'w_in': nrm(ks[8], (DEPTH, D_MODEL, D_IN), D_MODEL ** -0.5),
        'q_norm_g': 1.0 + nrm(ks[9], (DEPTH, HEAD_DIM), 0.05),
        'k_norm_g': 1.0 + nrm(ks[10], (DEPTH, HEAD_DIM), 0.05),
        'conv_w': nrm(ks[11], (DEPTH, CONV_W, D_RNN), CONV_W ** -0.5),
        'conv_b': nrm(ks[12], (DEPTH, D_RNN), 0.02),
        'w_rgate': nrm(ks[13], (DEPTH, RNN_BLOCKS, RNN_BLOCK_W, RNN_BLOCK_W), RNN_BLOCK_W ** -0.5),
        'b_rgate': nrm(ks[14], (DEPTH, RNN_BLOCKS, RNN_BLOCK_W), 0.02),
        'w_igate': nrm(ks[15], (DEPTH, RNN_BLOCKS, RNN_BLOCK_W, RNN_BLOCK_W), RNN_BLOCK_W ** -0.5),
        'b_igate': nrm(ks[16], (DEPTH, RNN_BLOCKS, RNN_BLOCK_W), 0.02),
        'lru_lambda': jnp.log(base) - jnp.log1p(-base),
        'w_out': nrm(ks[18], (DEPTH, D_ATT + D_RNN, D_MODEL), (D_ATT + D_RNN) ** -0.5),
        'norm2_g': 1.0 + nrm(ks[19], (DEPTH, D_MODEL), 0.05),
        'peer_w_q': nrm(ks[20], (DEPTH, D_MODEL, PEER_HEADS * PEER_QDIM), D_MODEL ** -0.5),
        'peer_keys': nrm(ks[21], (DEPTH, PEER_HEADS, 2, PEER_KEYS, PEER_HALF), PEER_HALF ** -0.5),
        'peer_u': nrm(ks[22], (DEPTH, PEER_EXPERTS, D_MODEL), D_MODEL ** -0.5),
        'peer_v': nrm(ks[23], (DEPTH, PEER_EXPERTS, D_MODEL), PEER_HEADS ** -0.5),
    }


def reference(x_prompt, x_sample, cache_win_k, cache_win_v, state_conv, state_rglru,
              rel_bias, norm1_g, w_in, q_norm_g, k_norm_g, conv_w, conv_b,
              w_rgate, b_rgate, w_igate, b_igate, lru_lambda, w_out, norm2_g,
              peer_w_q, peer_keys, peer_u, peer_v):
    yp, ys = x_prompt, x_sample
    pk, pv, pc, ph = [], [], [], []
    sk, sv, sc, sh = [], [], [], []
    for l in range(DEPTH):
        q, k, v, xr, gate = _mixer_inputs(yp, norm1_g[l], w_in[l], q_norm_g[l], k_norm_g[l])
        attn = _dilated_attn_prompt(q, k, v, rel_bias)
        keep = min(MAX_WINDOW, yp.shape[1])
        pk.append(k[:, -keep:])
        pv.append(v[:, -keep:])
        zc = jnp.zeros((yp.shape[0], CONV_W - 1, D_RNN), yp.dtype)
        zh = jnp.zeros((yp.shape[0], D_RNN), yp.dtype)
        y_rnn, c_new, h_new = _rglru_mixer(xr, gate, zc, zh, conv_w[l], conv_b[l], w_rgate[l],
                                           b_rgate[l], w_igate[l], b_igate[l], lru_lambda[l])
        pc.append(c_new)
        ph.append(h_new)
        yp = _layer_out(yp, attn, y_rnn, w_out[l], norm2_g[l], peer_w_q[l], peer_keys[l],
                        peer_u[l], peer_v[l])
        q, k, v, xr, gate = _mixer_inputs(ys, norm1_g[l], w_in[l], q_norm_g[l], k_norm_g[l])
        attn, kb_new, vb_new = _dilated_attn_sample(q, k, v, cache_win_k[l], cache_win_v[l], rel_bias)
        sk.append(kb_new)
        sv.append(vb_new)
        y_rnn, c_new, h_new = _rglru_mixer(xr, gate, state_conv[l], state_rglru[l], conv_w[l],
                                           conv_b[l], w_rgate[l], b_rgate[l], w_igate[l],
                                           b_igate[l], lru_lambda[l])
        sc.append(c_new)
        sh.append(h_new)
        ys = _layer_out(ys, attn, y_rnn, w_out[l], norm2_g[l], peer_w_q[l], peer_keys[l],
                        peer_u[l], peer_v[l])
    return (yp, ys, jnp.stack(pk), jnp.stack(pv), jnp.stack(pc), jnp.stack(ph),
            jnp.stack(sk), jnp.stack(sv), jnp.stack(sc), jnp.stack(sh))
```

```python
import functools
import math

import jax
import jax.numpy as jnp
from jax import lax
from jax.experimental import pallas as pl
from jax.experimental.pallas import tpu as pltpu

D_MODEL = 4096
D_ATT = 2048
HEAD_DIM = 128
N_ATT_HEADS = 16
DILATED_BRANCHES = ((128, 1), (512, 4), (2048, 16))
BRANCH_STEPS = 128
N_BUCKETS = 32
BUCKET_MAX_DIST = 2048
D_RNN = 2048
RNN_BLOCKS = 16
RNN_BLOCK_W = 128
CONV_W = 4
LRU_C = 8.0
D_IN = 3 * D_ATT + 2 * D_RNN
PEER_HEADS = 8
PEER_KEYS = 128
PEER_EXPERTS = PEER_KEYS * PEER_KEYS
PEER_TOPK = 16
PEER_QDIM = 256
RMS_EPS = 1e-6
NEG_INF = -1e30

V7X_LANES = 128
V7X_SUBLANES = 8
V7X_VMEM_BYTES = 64 * 1024 * 1024

_BF16 = jnp.bfloat16
_F32 = jnp.float32


def _params(semantics, vmem_mb):
    return pltpu.CompilerParams(dimension_semantics=semantics,
                                vmem_limit_bytes=vmem_mb * 1024 * 1024)


def _dot(a, b):
    return jnp.dot(a, b, preferred_element_type=_F32)


def _dot_nt(a, b, precision=None):
    return lax.dot_general(a, b, (((1,), (1,)), ((), ())), precision=precision,
                           preferred_element_type=_F32)


def _rms(x, g):
    return x * lax.rsqrt(jnp.mean(x * x, axis=-1, keepdims=True) + RMS_EPS) * g


def _rmsnorm_kernel(x_ref, g_ref, o_ref):
    o_ref[...] = _rms(x_ref[...], g_ref[...]).astype(o_ref.dtype)


def _rmsnorm_bf16(x, g):
    m, d = x.shape
    tm = min(m, 256)
    return pl.pallas_call(
        _rmsnorm_kernel,
        grid=(m // tm,),
        in_specs=[pl.BlockSpec((tm, d), lambda i: (i, 0)),
                  pl.BlockSpec((1, d), lambda i: (0, 0))],
        out_specs=pl.BlockSpec((tm, d), lambda i: (i, 0)),
        out_shape=jax.ShapeDtypeStruct((m, d), _BF16),
        compiler_params=_params(("parallel",), 32),
        name="rmsnorm",
    )(x, g.reshape(1, d))


def _matmul_kernel(a_ref, b_ref, o_ref):
    o_ref[...] = _dot(a_ref[...], b_ref[...])


def _matmul(a, b, name):
    m, k = a.shape
    n = b.shape[1]
    tm = min(m, 1024)
    tn = 512
    return pl.pallas_call(
        _matmul_kernel,
        grid=(m // tm, n // tn),
        in_specs=[pl.BlockSpec((tm, k), lambda i, j: (i, 0)),
                  pl.BlockSpec((k, tn), lambda i, j: (0, j))],
        out_specs=pl.BlockSpec((tm, tn), lambda i, j: (i, j)),
        out_shape=jax.ShapeDtypeStruct((m, n), _F32),
        compiler_params=_params(("parallel", "arbitrary"), 48),
        name=name,
    )(a, b)


def _out_proj_kernel(a1_ref, a2_ref, w1_ref, w2_ref, x_ref, o_ref):
    o_ref[...] = x_ref[...] + _dot(a1_ref[...].astype(_BF16), w1_ref[...]) + _dot(a2_ref[...], w2_ref[...])


def _out_proj(attn, y_rnn, w_out_bf16, x):
    m = x.shape[0]
    tm = min(m, 1024)
    tn = 512
    return pl.pallas_call(
        _out_proj_kernel,
        grid=(m // tm, D_MODEL // tn),
        in_specs=[pl.BlockSpec((tm, D_ATT), lambda i, j: (i, 0)),
                  pl.BlockSpec((tm, D_RNN), lambda i, j: (i, 0)),
                  pl.BlockSpec((D_ATT, tn), lambda i, j: (0, j)),
                  pl.BlockSpec((D_RNN, tn), lambda i, j: (1, j)),
                  pl.BlockSpec((tm, tn), lambda i, j: (i, j))],
        out_specs=pl.BlockSpec((tm, tn), lambda i, j: (i, j)),
        out_shape=jax.ShapeDtypeStruct((m, D_MODEL), _F32),
        compiler_params=_params(("parallel", "arbitrary"), 48),
        name="out_proj",
    )(attn, y_rnn, w_out_bf16, w_out_bf16, x)


def _add_kernel(a_ref, b_ref, o_ref):
    o_ref[...] = a_ref[...] + b_ref[...]


def _add(a, b):
    m, d = a.shape
    tm = min(m, 256)
    return pl.pallas_call(
        _add_kernel,
        grid=(m // tm,),
        in_specs=[pl.BlockSpec((tm, d), lambda i: (i, 0)),
                  pl.BlockSpec((tm, d), lambda i: (i, 0))],
        out_specs=pl.BlockSpec((tm, d), lambda i: (i, 0)),
        out_shape=jax.ShapeDtypeStruct((m, d), _F32),
        compiler_params=_params(("parallel",), 32),
        name="residual_add",
    )(a, b)


def _t5_bucket(dist):
    max_exact = N_BUCKETS // 2
    d_f = jnp.maximum(dist, max_exact).astype(_F32)
    large = max_exact + (jnp.log(d_f / max_exact) / math.log(BUCKET_MAX_DIST / max_exact)
                         * (N_BUCKETS - max_exact)).astype(jnp.int32)
    large = jnp.minimum(large, N_BUCKETS - 1)
    return jnp.where(dist < max_exact, dist, large)


def _prompt_bias(rel_bias):
    qi = jnp.arange(BRANCH_STEPS)[:, None]
    ki = jnp.arange(2 * BRANCH_STEPS)[None, :]
    delta = qi - ki + BRANCH_STEPS
    mask = (delta >= 0) & (delta <= BRANCH_STEPS)
    out = []
    for _, dil in DILATED_BRANCHES:
        b = rel_bias[_t5_bucket(jnp.clip(delta, 0, BRANCH_STEPS) * dil)].astype(_F32)
        out.append(jnp.where(mask[None], jnp.moveaxis(b, -1, 0), NEG_INF))
    return jnp.stack(out, axis=1)


def _sample_bias(rel_bias, n_q, buf):
    qpos = buf + jnp.arange(n_q)[:, None]
    cpos = jnp.arange(buf + n_q)[None, :]
    dist = qpos - cpos
    b = jnp.moveaxis(rel_bias[_t5_bucket(jnp.maximum(dist, 0))].astype(_F32), -1, 0)
    out = []
    for window, dil in DILATED_BRANCHES:
        mask = (dist >= 0) & (dist % dil == 0) & (dist <= window)
        out.append(jnp.where(mask[None], b, NEG_INF))
    return jnp.stack(out, axis=1)


def _attn_prompt_kernel(q_ref, k_ref, v_ref, bias_ref, gq_ref, gk_ref,
                        o_ref, ko_ref, vo_ref, qn_s, m_s, l_s, acc_s, *, seq):
    qn_s[...] = _rms(q_ref[...], gq_ref[...])
    ko_ref[...] = _rms(k_ref[...], gk_ref[...])
    vo_ref[...] = v_ref[...]
    m_s[...] = jnp.full(m_s.shape, NEG_INF, _F32)
    l_s[...] = jnp.zeros(l_s.shape, _F32)
    acc_s[...] = jnp.zeros(acc_s.shape, _F32)
    scale = HEAD_DIM ** -0.5

    for br, (_, dil) in enumerate(DILATED_BRANCHES):
        span = BRANCH_STEPS * dil
        log_dil = dil.bit_length() - 1

        def body(i, carry, br=br, dil=dil, span=span, log_dil=log_dil):
            r = jnp.bitwise_and(i, dil - 1)
            j = jnp.right_shift(i, log_dil)
            start = r + j * span
            pstart = r + jnp.maximum(j - 1, 0) * span
            rows = pl.ds(start, BRANCH_STEPS, stride=dil)
            prows = pl.ds(pstart, BRANCH_STEPS, stride=dil)
            qb = qn_s[rows, :].astype(_BF16)
            kc = ko_ref[rows, :].astype(_BF16)
            kp = ko_ref[prows, :].astype(_BF16)
            vc = v_ref[rows, :].astype(_BF16)
            vp = v_ref[prows, :].astype(_BF16)
            first = jnp.where(j == 0, NEG_INF, 0.0).astype(_F32)
            lc = _dot_nt(qb, kc) * scale + bias_ref[0, br, :, BRANCH_STEPS:]
            lp = _dot_nt(qb, kp) * scale + (bias_ref[0, br, :, :BRANCH_STEPS] + first)
            m_old = m_s[rows, :]
            m_new = jnp.maximum(m_old, jnp.maximum(jnp.max(lc, axis=-1, keepdims=True),
                                                   jnp.max(lp, axis=-1, keepdims=True)))
            pc = jnp.exp(lc - m_new)
            pp = jnp.exp(lp - m_new)
            alpha = jnp.exp(m_old - m_new)
            l_s[rows, :] = alpha * l_s[rows, :] + (jnp.sum(pc, axis=-1, keepdims=True)
                                                  + jnp.sum(pp, axis=-1, keepdims=True))
            acc_s[rows, :] = alpha * acc_s[rows, :] + (_dot(pc.astype(_BF16), vc)
                                                      + _dot(pp.astype(_BF16), vp))
            m_s[rows, :] = m_new
            return carry

        lax.fori_loop(0, seq // BRANCH_STEPS, body, 0)

    o_ref[...] = (acc_s[...] / l_s[...]).astype(o_ref.dtype)


def _attn_prompt(proj, bias, gq, gk, batch, seq):
    t = batch * seq
    col = lambda base: (lambda b, h: (b, base + h))
    blk = (seq, HEAD_DIM)
    out_spec = pl.BlockSpec(blk, lambda b, h: (b, h))
    return pl.pallas_call(
        functools.partial(_attn_prompt_kernel, seq=seq),
        grid=(batch, N_ATT_HEADS),
        in_specs=[pl.BlockSpec(blk, col(0)),
                  pl.BlockSpec(blk, col(N_ATT_HEADS)),
                  pl.BlockSpec(blk, col(2 * N_ATT_HEADS)),
                  pl.BlockSpec((1, len(DILATED_BRANCHES), BRANCH_STEPS, 2 * BRANCH_STEPS),
                               lambda b, h: (h, 0, 0, 0)),
                  pl.BlockSpec((1, HEAD_DIM), lambda b, h: (0, 0)),
                  pl.BlockSpec((1, HEAD_DIM), lambda b, h: (0, 0))],
        out_specs=[out_spec, out_spec, out_spec],
        out_shape=[jax.ShapeDtypeStruct((t, D_ATT), _BF16),
                   jax.ShapeDtypeStruct((t, D_ATT), _F32),
                   jax.ShapeDtypeStruct((t, D_ATT), _F32)],
        scratch_shapes=[pltpu.VMEM((seq, HEAD_DIM), _F32),
                        pltpu.VMEM((seq, 1), _F32),
                        pltpu.VMEM((seq, 1), _F32),
                        pltpu.VMEM((seq, HEAD_DIM), _F32)],
        compiler_params=_params(("parallel", "arbitrary"), 40),
        name="attn_prompt",
    )(proj, proj, proj, bias, gq.reshape(1, HEAD_DIM), gk.reshape(1, HEAD_DIM))


_SAMPLE_HEADS_PER_STEP = 4


def _attn_sample_kernel(q_ref, k_ref, v_ref, ck_ref, cv_ref, bc_ref, bn_ref, gq_ref, gk_ref,
                        o_ref, cko_ref, cvo_ref, *, n_q, buf):
    scale = HEAD_DIM ** -0.5
    for hh in range(_SAMPLE_HEADS_PER_STEP):
        cols = slice(hh * HEAD_DIM, (hh + 1) * HEAD_DIM)
        qn = _rms(q_ref[:, cols], gq_ref[...]).astype(_BF16)
        kn = _rms(k_ref[:, cols], gk_ref[...])
        vn = v_ref[:, cols]
        kc = ck_ref[0, :, cols]
        vc = cv_ref[0, :, cols]
        lc = _dot_nt(qn, kc.astype(_BF16)) * scale
        ln = _dot_nt(qn, kn.astype(_BF16)) * scale
        lcb = [lc + bc_ref[hh, br] for br in range(len(DILATED_BRANCHES))]
        lnb = [ln + bn_ref[hh, br] for br in range(len(DILATED_BRANCHES))]
        m = None
        for x in lcb + lnb:
            mx = jnp.max(x, axis=-1, keepdims=True)
            m = mx if m is None else jnp.maximum(m, mx)
        pc = sum(jnp.exp(x - m) for x in lcb)
        pn = sum(jnp.exp(x - m) for x in lnb)
        l = jnp.sum(pc, axis=-1, keepdims=True) + jnp.sum(pn, axis=-1, keepdims=True)
        o = _dot(pc.astype(_BF16), vc.astype(_BF16)) + _dot(pn.astype(_BF16), vn.astype(_BF16))
        o_ref[:, cols] = o / l
        cko_ref[0, :buf - n_q, cols] = ck_ref[0, n_q:, cols]
        cko_ref[0, buf - n_q:, cols] = kn
        cvo_ref[0, :buf - n_q, cols] = cv_ref[0, n_q:, cols]
        cvo_ref[0, buf - n_q:, cols] = vn


def _attn_sample(proj, cache_k, cache_v, bias_c, bias_n, gq, gk, batch, n_q):
    buf = cache_k.shape[1]
    hps = _SAMPLE_HEADS_PER_STEP
    width = hps * HEAD_DIM
    groups = N_ATT_HEADS // hps
    n_br = len(DILATED_BRANCHES)
    col = lambda base: (lambda b, g: (b, base + g))
    cache_spec = pl.BlockSpec((1, buf, width), lambda b, g: (b, 0, g))
    return pl.pallas_call(
        functools.partial(_attn_sample_kernel, n_q=n_q, buf=buf),
        grid=(batch, groups),
        in_specs=[pl.BlockSpec((n_q, width), col(0)),
                  pl.BlockSpec((n_q, width), col(groups)),
                  pl.BlockSpec((n_q, width), col(2 * groups)),
                  cache_spec, cache_spec,
                  pl.BlockSpec((hps, n_br, n_q, buf), lambda b, g: (g, 0, 0, 0)),
                  pl.BlockSpec((hps, n_br, n_q, n_q), lambda b, g: (g, 0, 0, 0)),
                  pl.BlockSpec((1, HEAD_DIM), lambda b, g: (0, 0)),
                  pl.BlockSpec((1, HEAD_DIM), lambda b, g: (0, 0))],
        out_specs=[pl.BlockSpec((n_q, width), lambda b, g: (b, g)), cache_spec, cache_spec],
        out_shape=[jax.ShapeDtypeStruct((batch * n_q, D_ATT), _F32),
                   jax.ShapeDtypeStruct(cache_k.shape, _F32),
                   jax.ShapeDtypeStruct(cache_v.shape, _F32)],
        compiler_params=_params(("parallel", "arbitrary"), 48),
        name="attn_sample",
    )(proj, proj, proj, cache_k, cache_v, bias_c, bias_n,
      gq.reshape(1, HEAD_DIM), gk.reshape(1, HEAD_DIM))


def _rglru_kernel(xr_ref, gate_ref, cprev_ref, hprev_ref, cw_ref, cb_ref, wr_ref, br_ref,
                  wi_ref, bi_ref, lam_ref, y_ref, cnew_ref, hnew_ref,
                  xpad_s, a_s, u_s, hs_s, *, n_seq, length):
    pad = V7X_SUBLANES
    z = -lam_ref[...]
    softplus = jnp.maximum(z, 0.0) + jnp.log1p(jnp.exp(-jnp.abs(z)))
    row = lax.broadcasted_iota(jnp.int32, (V7X_SUBLANES, RNN_BLOCK_W), 0)

    for s in range(n_seq):
        rows = slice(s * length, (s + 1) * length)
        xpad_s[pad - (CONV_W - 1):pad, :] = cprev_ref[s]
        xpad_s[pad:, :] = xr_ref[rows, :]
        xc = cb_ref[...] + sum(xpad_s[pad - (CONV_W - 1) + j:pad - (CONV_W - 1) + j + length, :]
                               * cw_ref[j:j + 1, :] for j in range(CONV_W))
        cnew_ref[s] = xpad_s[length + pad - (CONV_W - 1):length + pad, :]
        xcb = xc.astype(_BF16)
        r = jax.nn.sigmoid(_dot(xcb, wr_ref[0]) + br_ref[0])
        i = jax.nn.sigmoid(_dot(xcb, wi_ref[0]) + bi_ref[0])
        log_a = -LRU_C * r * softplus
        a_s[...] = jnp.exp(log_a)
        u_s[...] = jnp.sqrt(1.0 - jnp.exp(2.0 * log_a)) * i * xc

        def group(g, h_prev):
            sl = pl.ds(pl.multiple_of(g * V7X_SUBLANES, V7X_SUBLANES), V7X_SUBLANES)
            a = a_s[sl, :]
            u = u_s[sl, :]
            for sh in (1, 2, 4):
                a_sh = jnp.where(row >= sh, pltpu.roll(a, sh, 0), 1.0)
                u_sh = jnp.where(row >= sh, pltpu.roll(u, sh, 0), 0.0)
                u = a * u_sh + u
                a = a * a_sh
            h = a * h_prev + u
            hs_s[sl, :] = h
            return h[V7X_SUBLANES - 1:, :]

        h_last = lax.fori_loop(0, length // V7X_SUBLANES, group, hprev_ref[s])
        hnew_ref[s] = h_last
        y_ref[rows, :] = (hs_s[...] * jax.nn.gelu(gate_ref[rows, :])).astype(y_ref.dtype)


def _rglru(proj, conv_prev, h_prev, conv_w, conv_b, w_r, b_r, w_i, b_i, lam, n_seq_total, length,
           seq_per_step):
    t = n_seq_total * length
    rows = seq_per_step * length
    xr_col = 3 * D_ATT // RNN_BLOCK_W
    gate_col = xr_col + RNN_BLOCKS
    vec = lambda a: a.reshape(1, D_RNN)
    vec_spec = pl.BlockSpec((1, RNN_BLOCK_W), lambda s, n: (0, n))
    blk_w = pl.BlockSpec((1, RNN_BLOCK_W, RNN_BLOCK_W), lambda s, n: (n, 0, 0))
    blk_b = pl.BlockSpec((1, 1, RNN_BLOCK_W), lambda s, n: (n, 0, 0))
    return pl.pallas_call(
        functools.partial(_rglru_kernel, n_seq=seq_per_step, length=length),
        grid=(n_seq_total // seq_per_step, RNN_BLOCKS),
        in_specs=[pl.BlockSpec((rows, RNN_BLOCK_W), lambda s, n: (s, xr_col + n)),
                  pl.BlockSpec((rows, RNN_BLOCK_W), lambda s, n: (s, gate_col + n)),
                  pl.BlockSpec((seq_per_step, CONV_W - 1, RNN_BLOCK_W), lambda s, n: (s, 0, n)),
                  pl.BlockSpec((seq_per_step, 1, RNN_BLOCK_W), lambda s, n: (s, 0, n)),
                  pl.BlockSpec((CONV_W, RNN_BLOCK_W), lambda s, n: (0, n)),
                  vec_spec, blk_w, blk_b, blk_w, blk_b, vec_spec],
        out_specs=[pl.BlockSpec((rows, RNN_BLOCK_W), lambda s, n: (s, n)),
                   pl.BlockSpec((seq_per_step, CONV_W - 1, RNN_BLOCK_W), lambda s, n: (s, 0, n)),
                   pl.BlockSpec((seq_per_step, 1, RNN_BLOCK_W), lambda s, n: (s, 0, n))],
        out_shape=[jax.ShapeDtypeStruct((t, D_RNN), _BF16),
                   jax.ShapeDtypeStruct((n_seq_total, CONV_W - 1, D_RNN), _F32),
                   jax.ShapeDtypeStruct((n_seq_total, 1, D_RNN), _F32)],
        scratch_shapes=[pltpu.VMEM((length + V7X_SUBLANES, RNN_BLOCK_W), _F32),
                        pltpu.VMEM((length, RNN_BLOCK_W), _F32),
                        pltpu.VMEM((length, RNN_BLOCK_W), _F32),
                        pltpu.VMEM((length, RNN_BLOCK_W), _F32)],
        compiler_params=_params(("parallel", "arbitrary"), 32),
        name="rglru",
    )(proj, proj, conv_prev, h_prev.reshape(n_seq_total, 1, D_RNN), conv_w, vec(conv_b),
      w_r.astype(_BF16), b_r.reshape(RNN_BLOCKS, 1, RNN_BLOCK_W),
      w_i.astype(_BF16), b_i.reshape(RNN_BLOCKS, 1, RNN_BLOCK_W), vec(lam))


def _extract_max(s):
    m = jnp.max(s, axis=0, keepdims=True)
    return m, jnp.where(s == m, -jnp.inf, s)


def _peer_topk_kernel(q_ref, keys_ref, s1_ref, e1_ref, t2_ref, e2_ref):
    q = q_ref[...]
    half = PEER_QDIM // 2
    s1 = _dot_nt(keys_ref[0, 0], q[:, :half], precision=lax.Precision.HIGHEST)
    s2 = _dot_nt(keys_ref[0, 1], q[:, half:], precision=lax.Precision.HIGHEST)
    tm = s1.shape[1]
    krow = lax.broadcasted_iota(jnp.int32, (PEER_TOPK, tm), 0)

    def top_values(s):
        vals = jnp.zeros((PEER_TOPK, tm), _F32)
        for k in range(PEER_TOPK):
            m, s = _extract_max(s)
            vals = jnp.where(krow == k, m, vals)
        return vals

    v1 = top_values(s1)
    v2 = top_values(s2)
    cand = jnp.concatenate([v1[a:a + 1, :] + v2 for a in range(PEER_TOPK)], axis=0)
    c_max, cand = _extract_max(cand)
    z = jnp.ones_like(c_max)
    c_k = c_max
    for _ in range(PEER_TOPK - 1):
        c_k, cand = _extract_max(cand)
        z = z + jnp.exp(c_k - c_max)
    c_next, _ = _extract_max(cand)
    theta = 0.5 * (c_k + c_next)
    s1_ref[0] = s1
    e1_ref[0] = jnp.exp(s1 - v1[0:1, :]) / z
    t2_ref[0] = s2 - theta
    e2_ref[0] = jnp.exp(s2 - v2[0:1, :])


def _peer_topk(qp, keys):
    t = qp.shape[0]
    tm = min(t, 256)
    spec = pl.BlockSpec((1, PEER_KEYS, tm), lambda i, h: (h, 0, i))
    shape = jax.ShapeDtypeStruct((PEER_HEADS, PEER_KEYS, t), _F32)
    return pl.pallas_call(
        _peer_topk_kernel,
        grid=(t // tm, PEER_HEADS),
        in_specs=[pl.BlockSpec((tm, PEER_QDIM), lambda i, h: (i, h)),
                  pl.BlockSpec((1, 2, PEER_KEYS, PEER_QDIM // 2), lambda i, h: (h, 0, 0, 0))],
        out_specs=[spec, spec, spec, spec],
        out_shape=[shape, shape, shape, shape],
        compiler_params=_params(("parallel", "arbitrary"), 32),
        name="peer_topk",
    )(qp, keys)


_PEER_EXPERT_TILE = 256


def _peer_experts_kernel(hn_ref, u_ref, v_ref, s1_ref, e1_ref, t2_ref, e2_ref, o_ref):
    j = pl.program_id(1)

    @pl.when(j == 0)
    def _():
        o_ref[...] = jnp.zeros(o_ref.shape, _F32)

    pre = _dot_nt(hn_ref[...], u_ref[...])
    act = 0.5 * pre * (1.0 + lax.erf(pre * (2.0 ** -0.5)))
    groups = _PEER_EXPERT_TILE // PEER_KEYS
    gates = []
    for g in range(groups):
        r = jnp.bitwise_and(j * groups + g, V7X_SUBLANES - 1)
        acc = None
        for h in range(PEER_HEADS):
            s1_row = s1_ref[h, 0, pl.ds(r, 1), :]
            e1_row = e1_ref[h, 0, pl.ds(r, 1), :]
            hit = (s1_row + t2_ref[h]) >= 0.0
            term = jnp.where(hit, e1_row * e2_ref[h], 0.0)
            acc = term if acc is None else acc + term
        gates.append(acc.T)
    gate = jnp.concatenate(gates, axis=1)
    o_ref[...] += _dot((act * gate).astype(_BF16), v_ref[...])


def _peer_experts(hn, u_bf16, v_bf16, s1, e1, t2, e2):
    t = hn.shape[0]
    tm = min(t, 512)
    te = _PEER_EXPERT_TILE
    groups = te // PEER_KEYS
    sub = V7X_SUBLANES
    row_view = lambda a: a.reshape(PEER_HEADS, PEER_KEYS // sub, sub, t)
    row_spec = pl.BlockSpec((PEER_HEADS, 1, sub, tm), lambda i, j: (0, (j * groups) // sub, 0, i))
    full_spec = pl.BlockSpec((PEER_HEADS, PEER_KEYS, tm), lambda i, j: (0, 0, i))
    return pl.pallas_call(
        _peer_experts_kernel,
        grid=(t // tm, PEER_EXPERTS // te),
        in_specs=[pl.BlockSpec((tm, D_MODEL), lambda i, j: (i, 0)),
                  pl.BlockSpec((te, D_MODEL), lambda i, j: (j, 0)),
                  pl.BlockSpec((te, D_MODEL), lambda i, j: (j, 0)),
                  row_spec, row_spec, full_spec, full_spec],
        out_specs=pl.BlockSpec((tm, D_MODEL), lambda i, j: (i, 0)),
        out_shape=jax.ShapeDtypeStruct((t, D_MODEL), _F32),
        compiler_params=_params(("parallel", "arbitrary"), 56),
        name="peer_experts",
    )(hn, u_bf16, v_bf16, row_view(s1), row_view(e1), t2, e2)


def _layer_tail(x2d, attn, y_rnn, w_out, norm2_g, w_q, keys, u_bf16, v_bf16):
    h = _out_proj(attn, y_rnn, w_out, x2d)
    hn = _rmsnorm_bf16(h, norm2_g)
    qp = _matmul(hn, w_q, "peer_query")
    s1, e1, t2, e2 = _peer_topk(qp, keys)
    ffn = _peer_experts(hn, u_bf16, v_bf16, s1, e1, t2, e2)
    return _add(h, ffn)


def kernel(x_prompt, x_sample, cache_win_k, cache_win_v, state_conv, state_rglru, rel_bias, norm1_g,
           w_in, q_norm_g, k_norm_g, conv_w, conv_b, w_rgate, b_rgate, w_igate, b_igate, lru_lambda,
           w_out, norm2_g, peer_w_q, peer_keys, peer_u, peer_v):
    batch, seq, _ = x_prompt.shape
    dec_batch, dec_seq, _ = x_sample.shape
    buf = cache_win_k.shape[2]
    assert norm1_g.shape[0] == 1, "single-layer trunk"
    assert seq % (BRANCH_STEPS * DILATED_BRANCHES[-1][1]) == 0 and seq <= 2048
    assert dec_seq == V7X_SUBLANES and buf >= DILATED_BRANCHES[-1][0]

    w_in_b = w_in[0].astype(_BF16)
    w_out_b = w_out[0].astype(_BF16)
    w_q_b = peer_w_q[0].astype(_BF16)
    u_b = peer_u[0].astype(_BF16)
    v_b = peer_v[0].astype(_BF16)
    tail = functools.partial(_layer_tail, w_out=w_out_b, norm2_g=norm2_g[0], w_q=w_q_b,
                             keys=peer_keys[0], u_bf16=u_b, v_bf16=v_b)
    rnn_w = (conv_w[0], conv_b[0], w_rgate[0], b_rgate[0], w_igate[0], b_igate[0], lru_lambda[0])

    xp = x_prompt.reshape(batch * seq, D_MODEL)
    proj_p = _matmul(_rmsnorm_bf16(xp, norm1_g[0]), w_in_b, "in_proj")
    attn_p, k_p, v_p = _attn_prompt(proj_p, _prompt_bias(rel_bias), q_norm_g[0], k_norm_g[0], batch, seq)
    y_rnn_p, conv_p, h_p = _rglru(proj_p, jnp.zeros((batch, CONV_W - 1, D_RNN), _F32),
                                  jnp.zeros((batch, D_RNN), _F32), *rnn_w,
                                  n_seq_total=batch, length=seq, seq_per_step=1)
    y_p = tail(xp, attn_p, y_rnn_p)

    xs = x_sample.reshape(dec_batch * dec_seq, D_MODEL)
    proj_s = _matmul(_rmsnorm_bf16(xs, norm1_g[0]), w_in_b, "in_proj")
    bias_s = _sample_bias(rel_bias, dec_seq, buf)
    attn_s, k_s, v_s = _attn_sample(proj_s, cache_win_k[0].reshape(dec_batch, buf, D_ATT),
                                    cache_win_v[0].reshape(dec_batch, buf, D_ATT),
                                    bias_s[..., :buf], bias_s[..., buf:], q_norm_g[0], k_norm_g[0],
                                    dec_batch, dec_seq)
    y_rnn_s, conv_s, h_s = _rglru(proj_s, state_conv[0], state_rglru[0], *rnn_w,
                                  n_seq_total=dec_batch, length=dec_seq, seq_per_step=dec_batch)
    y_s = tail(xs, attn_s, y_rnn_s)

    heads = (N_ATT_HEADS, HEAD_DIM)
    return (y_p.reshape(batch, seq, D_MODEL), y_s.reshape(dec_batch, dec_seq, D_MODEL),
            k_p.reshape(1, batch, seq, *heads), v_p.reshape(1, batch, seq, *heads),
            conv_p[None], h_p.reshape(1, batch, D_RNN),
            k_s.reshape(1, dec_batch, buf, *heads), v_s.reshape(1, dec_batch, buf, *heads),
            conv_s[None], h_s.reshape(1, dec_batch, D_RNN))
```

```python
import functools
import math

import jax
import jax.numpy as jnp
from jax import lax
from jax.experimental import pallas as pl
from jax.experimental.pallas import tpu as pltpu

D_MODEL = 4096
D_ATT = 2048
HEAD_DIM = 128
N_ATT_HEADS = 16
DILATED_BRANCHES = ((128, 1), (512, 4), (2048, 16))
BRANCH_STEPS = 128
N_BUCKETS = 32
BUCKET_MAX_DIST = 2048
D_RNN = 2048
RNN_BLOCKS = 16
RNN_BLOCK_W = 128
CONV_W = 4
LRU_C = 8.0
D_IN = 3 * D_ATT + 2 * D_RNN
PEER_HEADS = 8
PEER_KEYS = 128
PEER_EXPERTS = PEER_KEYS * PEER_KEYS
PEER_TOPK = 16
PEER_QDIM = 256
RMS_EPS = 1e-6
NEG_INF = -1e30

V7X_LANES = 128
V7X_SUBLANES = 8
V7X_VMEM_BYTES = 64 * 1024 * 1024

_BF16 = jnp.bfloat16
_F32 = jnp.float32


def _params(semantics, vmem_mb):
    return pltpu.CompilerParams(dimension_semantics=semantics,
                                vmem_limit_bytes=vmem_mb * 1024 * 1024)


def _dot(a, b):
    return jnp.dot(a, b, preferred_element_type=_F32)


def _dot_nt(a, b, precision=None):
    return lax.dot_general(a, b, (((1,), (1,)), ((), ())), precision=precision,
                           preferred_element_type=_F32)


def _rms(x, g):
    return x * lax.rsqrt(jnp.mean(x * x, axis=-1, keepdims=True) + RMS_EPS) * g


def _rmsnorm_kernel(x_ref, g_ref, o_ref):
    o_ref[...] = _rms(x_ref[...], g_ref[...]).astype(o_ref.dtype)


def _rmsnorm_bf16(x, g):
    m, d = x.shape
    tm = min(m, 256)
    return pl.pallas_call(
        _rmsnorm_kernel,
        grid=(m // tm,),
        in_specs=[pl.BlockSpec((tm, d), lambda i: (i, 0)),
                  pl.BlockSpec((1, d), lambda i: (0, 0))],
        out_specs=pl.BlockSpec((tm, d), lambda i: (i, 0)),
        out_shape=jax.ShapeDtypeStruct((m, d), _BF16),
        compiler_params=_params(("parallel",), 32),
        name="rmsnorm",
    )(x, g.reshape(1, d))


def _matmul_kernel(a_ref, b_ref, o_ref):
    o_ref[...] = _dot(a_ref[...], b_ref[...])


def _matmul(a, b, name):
    m, k = a.shape
    n = b.shape[1]
    tm = min(m, 1024)
    tn = 512
    return pl.pallas_call(
        _matmul_kernel,
        grid=(m // tm, n // tn),
        in_specs=[pl.BlockSpec((tm, k), lambda i, j: (i, 0)),
                  pl.BlockSpec((k, tn), lambda i, j: (0, j))],
        out_specs=pl.BlockSpec((tm, tn), lambda i, j: (i, j)),
        out_shape=jax.ShapeDtypeStruct((m, n), _F32),
        compiler_params=_params(("parallel", "arbitrary"), 48),
        name=name,
    )(a, b)


def _out_proj_kernel(a1_ref, a2_ref, w1_ref, w2_ref, x_ref, o_ref):
    o_ref[...] = x_ref[...] + _dot(a1_ref[...].astype(_BF16), w1_ref[...]) + _dot(a2_ref[...], w2_ref[...])


def _out_proj(attn, y_rnn, w_out_bf16, x):
    m = x.shape[0]
    tm = min(m, 1024)
    tn = 512
    return pl.pallas_call(
        _out_proj_kernel,
        grid=(m // tm, D_MODEL // tn),
        in_specs=[pl.BlockSpec((tm, D_ATT), lambda i, j: (i, 0)),
                  pl.BlockSpec((tm, D_RNN), lambda i, j: (i, 0)),
                  pl.BlockSpec((D_ATT, tn), lambda i, j: (0, j)),
                  pl.BlockSpec((D_RNN, tn), lambda i, j: (1, j)),
                  pl.BlockSpec((tm, tn), lambda i, j: (i, j))],
        out_specs=pl.BlockSpec((tm, tn), lambda i, j: (i, j)),
        out_shape=jax.ShapeDtypeStruct((m, D_MODEL), _F32),
        compiler_params=_params(("parallel", "arbitrary"), 48),
        name="out_proj",
    )(attn, y_rnn, w_out_bf16, w_out_bf16, x)


def _add_kernel(a_ref, b_ref, o_ref):
    o_ref[...] = a_ref[...] + b_ref[...]


def _add(a, b):
    m, d = a.shape
    tm = min(m, 256)
    return pl.pallas_call(
        _add_kernel,
        grid=(m // tm,),
        in_specs=[pl.BlockSpec((tm, d), lambda i: (i, 0)),
                  pl.BlockSpec((tm, d), lambda i: (i, 0))],
        out_specs=pl.BlockSpec((tm, d), lambda i: (i, 0)),
        out_shape=jax.ShapeDtypeStruct((m, d), _F32),
        compiler_params=_params(("parallel",), 32),
        name="residual_add",
    )(a, b)


def _t5_bucket(dist):
    max_exact = N_BUCKETS // 2
    d_f = jnp.maximum(dist, max_exact).astype(_F32)
    large = max_exact + (jnp.log(d_f / max_exact) / math.log(BUCKET_MAX_DIST / max_exact)
                         * (N_BUCKETS - max_exact)).astype(jnp.int32)
    large = jnp.minimum(large, N_BUCKETS - 1)
    return jnp.where(dist < max_exact, dist, large)


def _prompt_bias(rel_bias):
    n = BRANCH_STEPS
    period = 3 * n
    shift = jnp.arange(-(n - 1), 2 * n)
    delta = n - shift
    valid = (delta >= 0) & (delta <= n)
    out = []
    for _, dil in DILATED_BRANCHES:
        g = rel_bias[_t5_bucket(jnp.clip(delta, 0, n) * dil)].astype(_F32)
        g = jnp.where(valid[:, None], g, NEG_INF).T
        ring = jnp.concatenate([g[:, n - 1:], g[:, :1], g[:, :n - 1]], axis=1)
        rows = jnp.tile(ring, (1, n))[:, :n * (period - 1)].reshape(-1, n, period - 1)
        out.append(rows[:, :, :2 * n])
    return jnp.stack(out, axis=1)


def _sample_bias(rel_bias, n_q, buf):
    ctx = buf + n_q
    dist = jnp.arange(ctx - 1, -1, -1)
    b = rel_bias[_t5_bucket(dist)].astype(_F32).T
    tail = jnp.full((b.shape[0], n_q - 1), NEG_INF, _F32)
    out = []
    for window, dil in DILATED_BRANCHES:
        mask = (dist % dil == 0) & (dist <= window)
        f = jnp.concatenate([jnp.where(mask[None], b, NEG_INF), tail], axis=1)
        out.append(jnp.stack([f[:, n_q - 1 - j:n_q - 1 - j + ctx] for j in range(n_q)], axis=1))
    return jnp.stack(out, axis=1)


_ATTN_UNROLL = 4


def _attn_prompt_kernel(q_ref, k_ref, v_ref, bias_ref, gq_ref, gk_ref,
                        o_ref, ko_ref, vo_ref, qn_s, ob_s, lse_s, *, seq):
    qn_s[...] = _rms(q_ref[...], gq_ref[...])
    ko_ref[...] = _rms(k_ref[...], gk_ref[...])
    vo_ref[...] = v_ref[...]
    scale = HEAD_DIM ** -0.5
    n_blocks = seq // BRANCH_STEPS
    max_dil = DILATED_BRANCHES[-1][1]

    for br, (_, dil) in enumerate(DILATED_BRANCHES):
        span = BRANCH_STEPS * dil
        log_dil = dil.bit_length() - 1
        blocks_per_residue = seq // span

        def blocks(it, carry, br=br, dil=dil, span=span, log_dil=log_dil,
                   blocks_per_residue=blocks_per_residue):
            has_prev = blocks_per_residue > 1
            ids = [it * _ATTN_UNROLL + u for u in range(_ATTN_UNROLL)]
            res = [jnp.bitwise_and(i, dil - 1) for i in ids]
            blk = [jnp.right_shift(i, log_dil) for i in ids]
            rows = [pl.ds(r + j * span, BRANCH_STEPS, stride=dil) for r, j in zip(res, blk)]
            prows = [pl.ds(r + jnp.maximum(j - 1, 0) * span, BRANCH_STEPS, stride=dil)
                     for r, j in zip(res, blk)]
            qb = [qn_s[rw, :].astype(_BF16) for rw in rows]
            lc = [_dot_nt(q, ko_ref[rw, :].astype(_BF16)) for q, rw in zip(qb, rows)]
            if has_prev:
                lp = [_dot_nt(q, ko_ref[rw, :].astype(_BF16)) for q, rw in zip(qb, prows)]
            pcs, pps, ms, ls = [], [], [], []
            for u in range(_ATTN_UNROLL):
                c = lc[u] * scale + bias_ref[0, br, :, BRANCH_STEPS:]
                m = jnp.max(c, axis=-1, keepdims=True)
                if has_prev:
                    first = jnp.where(blk[u] == 0, NEG_INF, 0.0).astype(_F32)
                    p = lp[u] * scale + (bias_ref[0, br, :, :BRANCH_STEPS] + first)
                    m = jnp.maximum(m, jnp.max(p, axis=-1, keepdims=True))
                    pp = jnp.exp(p - m)
                    pps.append(pp)
                pc = jnp.exp(c - m)
                l = jnp.sum(pc, axis=-1, keepdims=True)
                if has_prev:
                    l = l + jnp.sum(pp, axis=-1, keepdims=True)
                pcs.append(pc)
                ms.append(m)
                ls.append(l)
            acc = [_dot(pc.astype(_BF16), v_ref[rw, :].astype(_BF16)) for pc, rw in zip(pcs, rows)]
            if has_prev:
                acc = [a + _dot(pp.astype(_BF16), v_ref[rw, :].astype(_BF16))
                       for a, pp, rw in zip(acc, pps, prows)]
            for u in range(_ATTN_UNROLL):
                dst = pl.ds(pl.multiple_of((res[u] * blocks_per_residue + blk[u]) * BRANCH_STEPS,
                                           BRANCH_STEPS), BRANCH_STEPS)
                ob_s[br, dst, :] = acc[u] / ls[u]
                lse_s[br, dst, :] = ms[u] + jnp.log(ls[u])
            return carry

        lax.fori_loop(0, n_blocks // _ATTN_UNROLL, blocks, 0)

    chunk = seq // max_dil

    def merge(r, carry):
        outs, lses = [], []
        for br, (_, dil) in enumerate(DILATED_BRANCHES):
            start = (jnp.bitwise_and(r, dil - 1) * (seq // dil)
                     + jnp.right_shift(r, dil.bit_length() - 1))
            step = max_dil // dil
            rows = pl.ds(start, chunk, stride=step) if step > 1 else pl.ds(start, chunk)
            outs.append(ob_s[br, rows, :])
            lses.append(lse_s[br, rows, :])
        top = functools.reduce(jnp.maximum, lses)
        ws = [jnp.exp(x - top) for x in lses]
        total = functools.reduce(lambda a, b: a + b, ws)
        mixed = functools.reduce(lambda a, b: a + b, [w * o for w, o in zip(ws, outs)])
        o_ref[pl.ds(r, chunk, stride=max_dil), :] = mixed / total
        return carry

    lax.fori_loop(0, max_dil, merge, 0, unroll=2)


def _attn_prompt(proj, bias, gq, gk, batch, seq):
    t = batch * seq
    n_br = len(DILATED_BRANCHES)
    col = lambda base: (lambda b, h: (b, base + h))
    blk = (seq, HEAD_DIM)
    out_spec = pl.BlockSpec(blk, lambda b, h: (b, h))
    out = jax.ShapeDtypeStruct((t, D_ATT), _F32)
    return pl.pallas_call(
        functools.partial(_attn_prompt_kernel, seq=seq),
        grid=(batch, N_ATT_HEADS),
        in_specs=[pl.BlockSpec(blk, col(0)),
                  pl.BlockSpec(blk, col(N_ATT_HEADS)),
                  pl.BlockSpec(blk, col(2 * N_ATT_HEADS)),
                  pl.BlockSpec((1, n_br, BRANCH_STEPS, 2 * BRANCH_STEPS), lambda b, h: (h, 0, 0, 0)),
                  pl.BlockSpec((1, HEAD_DIM), lambda b, h: (0, 0)),
                  pl.BlockSpec((1, HEAD_DIM), lambda b, h: (0, 0))],
        out_specs=[out_spec, out_spec, out_spec],
        out_shape=[out, out, out],
        scratch_shapes=[pltpu.VMEM((seq, HEAD_DIM), _F32),
                        pltpu.VMEM((n_br, seq, HEAD_DIM), _F32),
                        pltpu.VMEM((n_br, seq, 1), _F32)],
        compiler_params=_params(("parallel", "arbitrary"), 40),
        name="attn_prompt",
    )(proj, proj, proj, bias, gq.reshape(1, HEAD_DIM), gk.reshape(1, HEAD_DIM))


_SAMPLE_HEADS_PER_STEP = 4


def _attn_sample_kernel(q_ref, k_ref, v_ref, ck_ref, cv_ref, bc_ref, bn_ref, gq_ref, gk_ref,
                        o_ref, cko_ref, cvo_ref, *, n_q, buf):
    scale = HEAD_DIM ** -0.5
    for hh in range(_SAMPLE_HEADS_PER_STEP):
        cols = slice(hh * HEAD_DIM, (hh + 1) * HEAD_DIM)
        qn = _rms(q_ref[:, cols], gq_ref[...]).astype(_BF16)
        kn = _rms(k_ref[:, cols], gk_ref[...])
        vn = v_ref[:, cols]
        kc = ck_ref[0, :, cols]
        vc = cv_ref[0, :, cols]
        lc = _dot_nt(qn, kc.astype(_BF16)) * scale
        ln = _dot_nt(qn, kn.astype(_BF16)) * scale
        lcb = [lc + bc_ref[hh, br] for br in range(len(DILATED_BRANCHES))]
        lnb = [ln + bn_ref[hh, br] for br in range(len(DILATED_BRANCHES))]
        m = None
        for x in lcb + lnb:
            mx = jnp.max(x, axis=-1, keepdims=True)
            m = mx if m is None else jnp.maximum(m, mx)
        pc = sum(jnp.exp(x - m) for x in lcb)
        pn = sum(jnp.exp(x - m) for x in lnb)
        l = jnp.sum(pc, axis=-1, keepdims=True) + jnp.sum(pn, axis=-1, keepdims=True)
        o = _dot(pc.astype(_BF16), vc.astype(_BF16)) + _dot(pn.astype(_BF16), vn.astype(_BF16))
        o_ref[:, cols] = o / l
        cko_ref[0, :buf - n_q, cols] = ck_ref[0, n_q:, cols]
        cko_ref[0, buf - n_q:, cols] = kn
        cvo_ref[0, :buf - n_q, cols] = cv_ref[0, n_q:, cols]
        cvo_ref[0, buf - n_q:, cols] = vn


def _attn_sample(proj, cache_k, cache_v, bias_c, bias_n, gq, gk, batch, n_q):
    buf = cache_k.shape[1]
    hps = _SAMPLE_HEADS_PER_STEP
    width = hps * HEAD_DIM
    groups = N_ATT_HEADS // hps
    n_br = len(DILATED_BRANCHES)
    col = lambda base: (lambda b, g: (b, base + g))
    cache_spec = pl.BlockSpec((1, buf, width), lambda b, g: (b, 0, g))
    return pl.pallas_call(
        functools.partial(_attn_sample_kernel, n_q=n_q, buf=buf),
        grid=(batch, groups),
        in_specs=[pl.BlockSpec((n_q, width), col(0)),
                  pl.BlockSpec((n_q, width), col(groups)),
                  pl.BlockSpec((n_q, width), col(2 * groups)),
                  cache_spec, cache_spec,
                  pl.BlockSpec((hps, n_br, n_q, buf), lambda b, g: (g, 0, 0, 0)),
                  pl.BlockSpec((hps, n_br, n_q, n_q), lambda b, g: (g, 0, 0, 0)),
                  pl.BlockSpec((1, HEAD_DIM), lambda b, g: (0, 0)),
                  pl.BlockSpec((1, HEAD_DIM), lambda b, g: (0, 0))],
        out_specs=[pl.BlockSpec((n_q, width), lambda b, g: (b, g)), cache_spec, cache_spec],
        out_shape=[jax.ShapeDtypeStruct((batch * n_q, D_ATT), _F32),
                   jax.ShapeDtypeStruct(cache_k.shape, _F32),
                   jax.ShapeDtypeStruct(cache_v.shape, _F32)],
        compiler_params=_params(("parallel", "arbitrary"), 48),
        name="attn_sample",
    )(proj, proj, proj, cache_k, cache_v, bias_c, bias_n,
      gq.reshape(1, HEAD_DIM), gk.reshape(1, HEAD_DIM))


def _rglru_kernel(xr_ref, gate_ref, cprev_ref, hprev_ref, cw_ref, cb_ref, wr_ref, br_ref,
                  wi_ref, bi_ref, lam_ref, y_ref, cnew_ref, hnew_ref,
                  xpad_s, a_s, u_s, hs_s, *, n_seq, length):
    pad = V7X_SUBLANES
    z = -lam_ref[...]
    softplus = jnp.maximum(z, 0.0) + jnp.log1p(jnp.exp(-jnp.abs(z)))
    row = lax.broadcasted_iota(jnp.int32, (V7X_SUBLANES, RNN_BLOCK_W), 0)

    for s in range(n_seq):
        rows = slice(s * length, (s + 1) * length)
        xpad_s[pad - (CONV_W - 1):pad, :] = cprev_ref[s]
        xpad_s[pad:, :] = xr_ref[rows, :]
        xc = cb_ref[...] + sum(xpad_s[pad - (CONV_W - 1) + j:pad - (CONV_W - 1) + j + length, :]
                               * cw_ref[j:j + 1, :] for j in range(CONV_W))
        cnew_ref[s] = xpad_s[length + pad - (CONV_W - 1):length + pad, :]
        xcb = xc.astype(_BF16)
        r = jax.nn.sigmoid(_dot(xcb, wr_ref[0]) + br_ref[0])
        i = jax.nn.sigmoid(_dot(xcb, wi_ref[0]) + bi_ref[0])
        log_a = -LRU_C * r * softplus
        a_s[...] = jnp.exp(log_a)
        u_s[...] = jnp.sqrt(1.0 - jnp.exp(2.0 * log_a)) * i * xc

        def group(g, h_prev):
            sl = pl.ds(pl.multiple_of(g * V7X_SUBLANES, V7X_SUBLANES), V7X_SUBLANES)
            a = a_s[sl, :]
            u = u_s[sl, :]
            for sh in (1, 2, 4):
                a_sh = jnp.where(row >= sh, pltpu.roll(a, sh, 0), 1.0)
                u_sh = jnp.where(row >= sh, pltpu.roll(u, sh, 0), 0.0)
                u = a * u_sh + u
                a = a * a_sh
            h = a * h_prev + u
            hs_s[sl, :] = h
            return h[V7X_SUBLANES - 1:, :]

        h_last = lax.fori_loop(0, length // V7X_SUBLANES, group, hprev_ref[s])
        hnew_ref[s] = h_last
        y_ref[rows, :] = (hs_s[...] * jax.nn.gelu(gate_ref[rows, :])).astype(y_ref.dtype)


def _rglru(proj, conv_prev, h_prev, conv_w, conv_b, w_r, b_r, w_i, b_i, lam, n_seq_total, length,
           seq_per_step):
    t = n_seq_total * length
    rows = seq_per_step * length
    xr_col = 3 * D_ATT // RNN_BLOCK_W
    gate_col = xr_col + RNN_BLOCKS
    vec = lambda a: a.reshape(1, D_RNN)
    vec_spec = pl.BlockSpec((1, RNN_BLOCK_W), lambda s, n: (0, n))
    blk_w = pl.BlockSpec((1, RNN_BLOCK_W, RNN_BLOCK_W), lambda s, n: (n, 0, 0))
    blk_b = pl.BlockSpec((1, 1, RNN_BLOCK_W), lambda s, n: (n, 0, 0))
    return pl.pallas_call(
        functools.partial(_rglru_kernel, n_seq=seq_per_step, length=length),
        grid=(n_seq_total // seq_per_step, RNN_BLOCKS),
        in_specs=[pl.BlockSpec((rows, RNN_BLOCK_W), lambda s, n: (s, xr_col + n)),
                  pl.BlockSpec((rows, RNN_BLOCK_W), lambda s, n: (s, gate_col + n)),
                  pl.BlockSpec((seq_per_step, CONV_W - 1, RNN_BLOCK_W), lambda s, n: (s, 0, n)),
                  pl.BlockSpec((seq_per_step, 1, RNN_BLOCK_W), lambda s, n: (s, 0, n)),
                  pl.BlockSpec((CONV_W, RNN_BLOCK_W), lambda s, n: (0, n)),
                  vec_spec, blk_w, blk_b, blk_w, blk_b, vec_spec],
        out_specs=[pl.BlockSpec((rows, RNN_BLOCK_W), lambda s, n: (s, n)),
                   pl.BlockSpec((seq_per_step, CONV_W - 1, RNN_BLOCK_W), lambda s, n: (s, 0, n)),
                   pl.BlockSpec((seq_per_step, 1, RNN_BLOCK_W), lambda s, n: (s, 0, n))],
        out_shape=[jax.ShapeDtypeStruct((t, D_RNN), _BF16),
                   jax.ShapeDtypeStruct((n_seq_total, CONV_W - 1, D_RNN), _F32),
                   jax.ShapeDtypeStruct((n_seq_total, 1, D_RNN), _F32)],
        scratch_shapes=[pltpu.VMEM((length + V7X_SUBLANES, RNN_BLOCK_W), _F32),
                        pltpu.VMEM((length, RNN_BLOCK_W), _F32),
                        pltpu.VMEM((length, RNN_BLOCK_W), _F32),
                        pltpu.VMEM((length, RNN_BLOCK_W), _F32)],
        compiler_params=_params(("parallel", "arbitrary"), 32),
        name="rglru",
    )(proj, proj, conv_prev, h_prev.reshape(n_seq_total, 1, D_RNN), conv_w, vec(conv_b),
      w_r.astype(_BF16), b_r.reshape(RNN_BLOCKS, 1, RNN_BLOCK_W),
      w_i.astype(_BF16), b_i.reshape(RNN_BLOCKS, 1, RNN_BLOCK_W), vec(lam))


def _extract_max(s):
    m = jnp.max(s, axis=0, keepdims=True)
    return m, jnp.where(s == m, -jnp.inf, s)


def _peer_topk_kernel(q_ref, keys_ref, s1_ref, e1_ref, t2_ref, e2_ref):
    q = q_ref[...]
    half = PEER_QDIM // 2
    s1 = _dot_nt(keys_ref[0, 0], q[:, :half], precision=lax.Precision.HIGHEST)
    s2 = _dot_nt(keys_ref[0, 1], q[:, half:], precision=lax.Precision.HIGHEST)
    tm = s1.shape[1]
    krow = lax.broadcasted_iota(jnp.int32, (PEER_TOPK, tm), 0)

    def top_values(s):
        vals = jnp.zeros((PEER_TOPK, tm), _F32)
        for k in range(PEER_TOPK):
            m, s = _extract_max(s)
            vals = jnp.where(krow == k, m, vals)
        return vals

    v1 = top_values(s1)
    v2 = top_values(s2)
    cand = jnp.concatenate([v1[a:a + 1, :] + v2 for a in range(PEER_TOPK)], axis=0)
    c_max, cand = _extract_max(cand)
    z = jnp.ones_like(c_max)
    c_k = c_max
    for _ in range(PEER_TOPK - 1):
        c_k, cand = _extract_max(cand)
        z = z + jnp.exp(c_k - c_max)
    c_next, _ = _extract_max(cand)
    theta = 0.5 * (c_k + c_next)
    s1_ref[0] = s1
    e1_ref[0] = jnp.exp(s1 - v1[0:1, :]) / z
    t2_ref[0] = s2 - theta
    e2_ref[0] = jnp.exp(s2 - v2[0:1, :])


def _peer_topk(qp, keys):
    t = qp.shape[0]
    tm = min(t, 256)
    spec = pl.BlockSpec((1, PEER_KEYS, tm), lambda i, h: (h, 0, i))
    shape = jax.ShapeDtypeStruct((PEER_HEADS, PEER_KEYS, t), _F32)
    return pl.pallas_call(
        _peer_topk_kernel,
        grid=(t // tm, PEER_HEADS),
        in_specs=[pl.BlockSpec((tm, PEER_QDIM), lambda i, h: (i, h)),
                  pl.BlockSpec((1, 2, PEER_KEYS, PEER_QDIM // 2), lambda i, h: (h, 0, 0, 0))],
        out_specs=[spec, spec, spec, spec],
        out_shape=[shape, shape, shape, shape],
        compiler_params=_params(("parallel", "arbitrary"), 32),
        name="peer_topk",
    )(qp, keys)


_PEER_EXPERT_TILE = 512


def _peer_experts_kernel(hn_ref, u_ref, v_ref, s1_ref, e1_ref, t2_ref, e2_ref, o_ref, pre_s, gate_s,
                         *, n_tiles):
    j = pl.program_id(1)
    groups = _PEER_EXPERT_TILE // PEER_KEYS

    @pl.when(j == 0)
    def _():
        o_ref[...] = jnp.zeros(o_ref.shape, _F32)
        pre_s[1] = jnp.zeros(pre_s.shape[1:], _F32)
        gate_s[1] = jnp.zeros(gate_s.shape[1:], _F32)

    slot = jnp.bitwise_and(j, 1)
    pre = pre_s[1 - slot]
    act = 0.5 * pre * (1.0 + lax.erf(pre * (2.0 ** -0.5)))
    o_ref[...] += _dot((act * gate_s[1 - slot]).astype(_BF16), v_ref[...])

    pre_s[slot] = _dot_nt(hn_ref[...], u_ref[...])
    tile = jnp.minimum(j, n_tiles - 1)
    for g in range(groups):
        r = jnp.bitwise_and(tile * groups + g, V7X_SUBLANES - 1)
        acc = None
        for h in range(PEER_HEADS):
            s1_row = s1_ref[h, 0, pl.ds(r, 1), :]
            e1_row = e1_ref[h, 0, pl.ds(r, 1), :]
            hit = (s1_row + t2_ref[h]) >= 0.0
            term = jnp.where(hit, e1_row * e2_ref[h], 0.0)
            acc = term if acc is None else acc + term
        gate_s[slot, :, g * PEER_KEYS:(g + 1) * PEER_KEYS] = acc.T


def _peer_experts(hn, u_bf16, v_bf16, s1, e1, t2, e2):
    t = hn.shape[0]
    tm = min(t, 512)
    te = _PEER_EXPERT_TILE
    n_tiles = PEER_EXPERTS // te
    groups = te // PEER_KEYS
    sub = V7X_SUBLANES
    prev = lambda j: jnp.maximum(j - 1, 0)
    cur = lambda j: jnp.minimum(j, n_tiles - 1)
    once = pl.Buffered(1)
    row_view = lambda a: a.reshape(PEER_HEADS, PEER_KEYS // sub, sub, t)
    row_spec = pl.BlockSpec((PEER_HEADS, 1, sub, tm), lambda i, j: (0, (cur(j) * groups) // sub, 0, i))
    full_spec = pl.BlockSpec((PEER_HEADS, PEER_KEYS, tm), lambda i, j: (0, 0, i), pipeline_mode=once)
    return pl.pallas_call(
        functools.partial(_peer_experts_kernel, n_tiles=n_tiles),
        grid=(t // tm, n_tiles + 1),
        in_specs=[pl.BlockSpec((tm, D_MODEL), lambda i, j: (i, 0), pipeline_mode=once),
                  pl.BlockSpec((te, D_MODEL), lambda i, j: (cur(j), 0)),
                  pl.BlockSpec((te, D_MODEL), lambda i, j: (prev(j), 0)),
                  row_spec, row_spec, full_spec, full_spec],
        out_specs=pl.BlockSpec((tm, D_MODEL), lambda i, j: (i, 0)),
        out_shape=jax.ShapeDtypeStruct((t, D_MODEL), _F32),
        scratch_shapes=[pltpu.VMEM((2, tm, te), _F32), pltpu.VMEM((2, tm, te), _F32)],
        compiler_params=_params(("parallel", "arbitrary"), 56),
        name="peer_experts",
    )(hn, u_bf16, v_bf16, row_view(s1), row_view(e1), t2, e2)


def _layer_tail(x2d, attn, y_rnn, w_out, norm2_g, w_q, keys, u_bf16, v_bf16):
    h = _out_proj(attn, y_rnn, w_out, x2d)
    hn = _rmsnorm_bf16(h, norm2_g)
    qp = _matmul(hn, w_q, "peer_query")
    s1, e1, t2, e2 = _peer_topk(qp, keys)
    ffn = _peer_experts(hn, u_bf16, v_bf16, s1, e1, t2, e2)
    return _add(h, ffn)


def kernel(x_prompt, x_sample, cache_win_k, cache_win_v, state_conv, state_rglru, rel_bias, norm1_g,
           w_in, q_norm_g, k_norm_g, conv_w, conv_b, w_rgate, b_rgate, w_igate, b_igate, lru_lambda,
           w_out, norm2_g, peer_w_q, peer_keys, peer_u, peer_v):
    batch, seq, _ = x_prompt.shape
    dec_batch, dec_seq, _ = x_sample.shape
    buf = cache_win_k.shape[2]
    assert norm1_g.shape[0] == 1, "single-layer trunk"
    assert seq % (BRANCH_STEPS * DILATED_BRANCHES[-1][1]) == 0 and seq <= 2048
    assert dec_seq == V7X_SUBLANES and buf >= DILATED_BRANCHES[-1][0]

    w_in_b = w_in[0].astype(_BF16)
    w_out_b = w_out[0].astype(_BF16)
    w_q_b = peer_w_q[0].astype(_BF16)
    u_b = peer_u[0].astype(_BF16)
    v_b = peer_v[0].astype(_BF16)
    tail = functools.partial(_layer_tail, w_out=w_out_b, norm2_g=norm2_g[0], w_q=w_q_b,
                             keys=peer_keys[0], u_bf16=u_b, v_bf16=v_b)
    rnn_w = (conv_w[0], conv_b[0], w_rgate[0], b_rgate[0], w_igate[0], b_igate[0], lru_lambda[0])

    xp = x_prompt.reshape(batch * seq, D_MODEL)
    proj_p = _matmul(_rmsnorm_bf16(xp, norm1_g[0]), w_in_b, "in_proj")
    attn_p, k_p, v_p = _attn_prompt(proj_p, _prompt_bias(rel_bias), q_norm_g[0], k_norm_g[0], batch, seq)
    y_rnn_p, conv_p, h_p = _rglru(proj_p, jnp.zeros((batch, CONV_W - 1, D_RNN), _F32),
                                  jnp.zeros((batch, D_RNN), _F32), *rnn_w,
                                  n_seq_total=batch, length=seq, seq_per_step=1)
    y_p = tail(xp, attn_p, y_rnn_p)

    xs = x_sample.reshape(dec_batch * dec_seq, D_MODEL)
    proj_s = _matmul(_rmsnorm_bf16(xs, norm1_g[0]), w_in_b, "in_proj")
    bias_s = _sample_bias(rel_bias, dec_seq, buf)
    attn_s, k_s, v_s = _attn_sample(proj_s, cache_win_k[0].reshape(dec_batch, buf, D_ATT),
                                    cache_win_v[0].reshape(dec_batch, buf, D_ATT),
                                    bias_s[..., :buf], bias_s[..., buf:], q_norm_g[0], k_norm_g[0],
                                    dec_batch, dec_seq)
    y_rnn_s, conv_s, h_s = _rglru(proj_s, state_conv[0], state_rglru[0], *rnn_w,
                                  n_seq_total=dec_batch, length=dec_seq, seq_per_step=dec_batch)
    y_s = tail(xs, attn_s, y_rnn_s)

    heads = (N_ATT_HEADS, HEAD_DIM)
    return (y_p.reshape(batch, seq, D_MODEL), y_s.reshape(dec_batch, dec_seq, D_MODEL),
            k_p.reshape(1, batch, seq, *heads), v_p.reshape(1, batch, seq, *heads),
            conv_p[None], h_p.reshape(1, batch, D_RNN),
            k_s.reshape(1, dec_batch, buf, *heads), v_s.reshape(1, dec_batch, buf, *heads),
            conv_s[None], h_s.reshape(1, dec_batch, D_RNN))
```

```python
import functools
import math

import jax
import jax.numpy as jnp
from jax import lax
from jax.experimental import pallas as pl
from jax.experimental.pallas import tpu as pltpu

D_MODEL = 4096
D_ATT = 2048
HEAD_DIM = 128
N_ATT_HEADS = 16
DILATED_BRANCHES = ((128, 1), (512, 4), (2048, 16))
BRANCH_STEPS = 128
N_BUCKETS = 32
BUCKET_MAX_DIST = 2048
D_RNN = 2048
RNN_BLOCKS = 16
RNN_BLOCK_W = 128
CONV_W = 4
LRU_C = 8.0
D_IN = 3 * D_ATT + 2 * D_RNN
PEER_HEADS = 8
PEER_KEYS = 128
PEER_EXPERTS = PEER_KEYS * PEER_KEYS
PEER_TOPK = 16
PEER_QDIM = 256
RMS_EPS = 1e-6
NEG_INF = -1e30

V7X_LANES = 128
V7X_SUBLANES = 8
V7X_VMEM_BYTES = 64 * 1024 * 1024

_BF16 = jnp.bfloat16
_F32 = jnp.float32


def _params(semantics, vmem_mb):
    return pltpu.CompilerParams(dimension_semantics=semantics,
                                vmem_limit_bytes=vmem_mb * 1024 * 1024)


def _dot(a, b):
    return jnp.dot(a, b, preferred_element_type=_F32)


def _dot_nt(a, b, precision=None):
    return lax.dot_general(a, b, (((1,), (1,)), ((), ())), precision=precision,
                           preferred_element_type=_F32)


def _rms(x, g):
    return x * lax.rsqrt(jnp.mean(x * x, axis=-1, keepdims=True) + RMS_EPS) * g


def _rmsnorm_kernel(x_ref, g_ref, o_ref):
    o_ref[...] = _rms(x_ref[...], g_ref[...]).astype(o_ref.dtype)


def _rmsnorm_bf16(x, g):
    m, d = x.shape
    tm = min(m, 256)
    return pl.pallas_call(
        _rmsnorm_kernel,
        grid=(m // tm,),
        in_specs=[pl.BlockSpec((tm, d), lambda i: (i, 0)),
                  pl.BlockSpec((1, d), lambda i: (0, 0))],
        out_specs=pl.BlockSpec((tm, d), lambda i: (i, 0)),
        out_shape=jax.ShapeDtypeStruct((m, d), _BF16),
        compiler_params=_params(("parallel",), 32),
        name="rmsnorm",
    )(x, g.reshape(1, d))


def _matmul_kernel(a_ref, b_ref, o_ref):
    o_ref[...] = _dot(a_ref[...], b_ref[...])


def _matmul(a, b, name):
    m, k = a.shape
    n = b.shape[1]
    tm = min(m, 1024)
    tn = 512
    return pl.pallas_call(
        _matmul_kernel,
        grid=(m // tm, n // tn),
        in_specs=[pl.BlockSpec((tm, k), lambda i, j: (i, 0)),
                  pl.BlockSpec((k, tn), lambda i, j: (0, j))],
        out_specs=pl.BlockSpec((tm, tn), lambda i, j: (i, j)),
        out_shape=jax.ShapeDtypeStruct((m, n), _F32),
        compiler_params=_params(("parallel", "arbitrary"), 48),
        name=name,
    )(a, b)


def _out_proj_kernel(a1_ref, a2_ref, w1_ref, w2_ref, x_ref, o_ref):
    o_ref[...] = x_ref[...] + _dot(a1_ref[...].astype(_BF16), w1_ref[...]) + _dot(a2_ref[...], w2_ref[...])


def _out_proj(attn, y_rnn, w_out_bf16, x):
    m = x.shape[0]
    tm = min(m, 1024)
    tn = 512
    return pl.pallas_call(
        _out_proj_kernel,
        grid=(m // tm, D_MODEL // tn),
        in_specs=[pl.BlockSpec((tm, D_ATT), lambda i, j: (i, 0)),
                  pl.BlockSpec((tm, D_RNN), lambda i, j: (i, 0)),
                  pl.BlockSpec((D_ATT, tn), lambda i, j: (0, j)),
                  pl.BlockSpec((D_RNN, tn), lambda i, j: (1, j)),
                  pl.BlockSpec((tm, tn), lambda i, j: (i, j))],
        out_specs=pl.BlockSpec((tm, tn), lambda i, j: (i, j)),
        out_shape=jax.ShapeDtypeStruct((m, D_MODEL), _F32),
        compiler_params=_params(("parallel", "arbitrary"), 48),
        name="out_proj",
    )(attn, y_rnn, w_out_bf16, w_out_bf16, x)


def _add_kernel(a_ref, b_ref, o_ref):
    o_ref[...] = a_ref[...] + b_ref[...]


def _add(a, b):
    m, d = a.shape
    tm = min(m, 256)
    return pl.pallas_call(
        _add_kernel,
        grid=(m // tm,),
        in_specs=[pl.BlockSpec((tm, d), lambda i: (i, 0)),
                  pl.BlockSpec((tm, d), lambda i: (i, 0))],
        out_specs=pl.BlockSpec((tm, d), lambda i: (i, 0)),
        out_shape=jax.ShapeDtypeStruct((m, d), _F32),
        compiler_params=_params(("parallel",), 32),
        name="residual_add",
    )(a, b)


def _t5_bucket(dist):
    max_exact = N_BUCKETS // 2
    d_f = jnp.maximum(dist, max_exact).astype(_F32)
    large = max_exact + (jnp.log(d_f / max_exact) / math.log(BUCKET_MAX_DIST / max_exact)
                         * (N_BUCKETS - max_exact)).astype(jnp.int32)
    large = jnp.minimum(large, N_BUCKETS - 1)
    return jnp.where(dist < max_exact, dist, large)


def _prompt_bias(rel_bias):
    n = BRANCH_STEPS
    period = 3 * n
    shift = jnp.arange(-(n - 1), 2 * n)
    delta = n - shift
    valid = (delta >= 0) & (delta <= n)
    out = []
    for _, dil in DILATED_BRANCHES:
        g = rel_bias[_t5_bucket(jnp.clip(delta, 0, n) * dil)].astype(_F32)
        g = jnp.where(valid[:, None], g, NEG_INF).T
        ring = jnp.concatenate([g[:, n - 1:], g[:, :1], g[:, :n - 1]], axis=1)
        rows = jnp.tile(ring, (1, n))[:, :n * (period - 1)].reshape(-1, n, period - 1)
        out.append(rows[:, :, :2 * n])
    return jnp.stack(out, axis=1)


def _sample_bias(rel_bias, n_q, buf):
    ctx = buf + n_q
    dist = jnp.arange(ctx - 1, -1, -1)
    b = rel_bias[_t5_bucket(dist)].astype(_F32).T
    tail = jnp.full((b.shape[0], n_q - 1), NEG_INF, _F32)
    out = []
    for window, dil in DILATED_BRANCHES:
        mask = (dist % dil == 0) & (dist <= window)
        f = jnp.concatenate([jnp.where(mask[None], b, NEG_INF), tail], axis=1)
        out.append(jnp.stack([f[:, n_q - 1 - j:n_q - 1 - j + ctx] for j in range(n_q)], axis=1))
    return jnp.stack(out, axis=1)


_ATTN_UNROLL = 4


def _attn_prompt_kernel(q_ref, k_ref, v_ref, bias_ref, gq_ref, gk_ref,
                        o_ref, ko_ref, vo_ref, qn_s, ob_s, lse_s, *, seq):
    qn_s[...] = _rms(q_ref[...], gq_ref[...])
    ko_ref[...] = _rms(k_ref[...], gk_ref[...])
    vo_ref[...] = v_ref[...]
    scale = HEAD_DIM ** -0.5
    n_blocks = seq // BRANCH_STEPS
    max_dil = DILATED_BRANCHES[-1][1]

    for br, (_, dil) in enumerate(DILATED_BRANCHES):
        span = BRANCH_STEPS * dil
        log_dil = dil.bit_length() - 1
        blocks_per_residue = seq // span

        def blocks(it, carry, br=br, dil=dil, span=span, log_dil=log_dil,
                   blocks_per_residue=blocks_per_residue):
            has_prev = blocks_per_residue > 1
            ids = [it * _ATTN_UNROLL + u for u in range(_ATTN_UNROLL)]
            res = [jnp.bitwise_and(i, dil - 1) for i in ids]
            blk = [jnp.right_shift(i, log_dil) for i in ids]
            rows = [pl.ds(r + j * span, BRANCH_STEPS, stride=dil) for r, j in zip(res, blk)]
            prows = [pl.ds(r + jnp.maximum(j - 1, 0) * span, BRANCH_STEPS, stride=dil)
                     for r, j in zip(res, blk)]
            qb = [qn_s[rw, :].astype(_BF16) for rw in rows]
            lc = [_dot_nt(q, ko_ref[rw, :].astype(_BF16)) for q, rw in zip(qb, rows)]
            if has_prev:
                lp = [_dot_nt(q, ko_ref[rw, :].astype(_BF16)) for q, rw in zip(qb, prows)]
            pcs, pps, ms, ls = [], [], [], []
            for u in range(_ATTN_UNROLL):
                c = lc[u] * scale + bias_ref[0, br, :, BRANCH_STEPS:]
                m = jnp.max(c, axis=-1, keepdims=True)
                if has_prev:
                    first = jnp.where(blk[u] == 0, NEG_INF, 0.0).astype(_F32)
                    p = lp[u] * scale + (bias_ref[0, br, :, :BRANCH_STEPS] + first)
                    m = jnp.maximum(m, jnp.max(p, axis=-1, keepdims=True))
                    pp = jnp.exp(p - m)
                    pps.append(pp)
                pc = jnp.exp(c - m)
                l = jnp.sum(pc, axis=-1, keepdims=True)
                if has_prev:
                    l = l + jnp.sum(pp, axis=-1, keepdims=True)
                pcs.append(pc)
                ms.append(m)
                ls.append(l)
            acc = [_dot(pc.astype(_BF16), v_ref[rw, :].astype(_BF16)) for pc, rw in zip(pcs, rows)]
            if has_prev:
                acc = [a + _dot(pp.astype(_BF16), v_ref[rw, :].astype(_BF16))
                       for a, pp, rw in zip(acc, pps, prows)]
            for u in range(_ATTN_UNROLL):
                dst = pl.ds(pl.multiple_of((res[u] * blocks_per_residue + blk[u]) * BRANCH_STEPS,
                                           BRANCH_STEPS), BRANCH_STEPS)
                ob_s[br, dst, :] = acc[u] / ls[u]
                lse_s[br, dst, :] = ms[u] + jnp.log(ls[u])
            return carry

        lax.fori_loop(0, n_blocks // _ATTN_UNROLL, blocks, 0)

    chunk = seq // max_dil

    def merge(r, carry):
        outs, lses = [], []
        for br, (_, dil) in enumerate(DILATED_BRANCHES):
            start = (jnp.bitwise_and(r, dil - 1) * (seq // dil)
                     + jnp.right_shift(r, dil.bit_length() - 1))
            step = max_dil // dil
            rows = pl.ds(start, chunk, stride=step) if step > 1 else pl.ds(start, chunk)
            outs.append(ob_s[br, rows, :])
            lses.append(lse_s[br, rows, :])
        top = functools.reduce(jnp.maximum, lses)
        ws = [jnp.exp(x - top) for x in lses]
        total = functools.reduce(lambda a, b: a + b, ws)
        mixed = functools.reduce(lambda a, b: a + b, [w * o for w, o in zip(ws, outs)])
        o_ref[pl.ds(r, chunk, stride=max_dil), :] = mixed / total
        return carry

    lax.fori_loop(0, max_dil, merge, 0, unroll=2)


def _attn_prompt(proj, bias, gq, gk, batch, seq):
    t = batch * seq
    n_br = len(DILATED_BRANCHES)
    col = lambda base: (lambda b, h: (b, base + h))
    blk = (seq, HEAD_DIM)
    out_spec = pl.BlockSpec(blk, lambda b, h: (b, h))
    out = jax.ShapeDtypeStruct((t, D_ATT), _F32)
    return pl.pallas_call(
        functools.partial(_attn_prompt_kernel, seq=seq),
        grid=(batch, N_ATT_HEADS),
        in_specs=[pl.BlockSpec(blk, col(0)),
                  pl.BlockSpec(blk, col(N_ATT_HEADS)),
                  pl.BlockSpec(blk, col(2 * N_ATT_HEADS)),
                  pl.BlockSpec((1, n_br, BRANCH_STEPS, 2 * BRANCH_STEPS), lambda b, h: (h, 0, 0, 0)),
                  pl.BlockSpec((1, HEAD_DIM), lambda b, h: (0, 0)),
                  pl.BlockSpec((1, HEAD_DIM), lambda b, h: (0, 0))],
        out_specs=[out_spec, out_spec, out_spec],
        out_shape=[out, out, out],
        scratch_shapes=[pltpu.VMEM((seq, HEAD_DIM), _F32),
                        pltpu.VMEM((n_br, seq, HEAD_DIM), _F32),
                        pltpu.VMEM((n_br, seq, 1), _F32)],
        compiler_params=_params(("parallel", "arbitrary"), 40),
        name="attn_prompt",
    )(proj, proj, proj, bias, gq.reshape(1, HEAD_DIM), gk.reshape(1, HEAD_DIM))


_SAMPLE_CHUNK = 512


def _attn_sample_kernel(q_ref, k_ref, v_ref, ck_ref, cv_ref, bc_ref, bn_ref, gq_ref, gk_ref,
                        o_ref, cko_ref, cvo_ref,
                        qn_s, m_s, l_s, acc_s, knew_s, vnew_s, sem, *, n_q, buf):
    b = pl.program_id(0)
    c = pl.program_id(1)
    chunk = _SAMPLE_CHUNK
    last = buf // chunk - 1
    scale = HEAD_DIM ** -0.5
    n_br = len(DILATED_BRANCHES)
    heads = range(N_ATT_HEADS)
    head_cols = [slice(h * HEAD_DIM, (h + 1) * HEAD_DIM) for h in heads]

    def shift_copies(first):
        nh = N_ATT_HEADS
        src = pl.ds(n_q * nh, (chunk - n_q) * nh) if first else pl.ds(0, chunk * nh)
        dst = pl.ds(0, (chunk - n_q) * nh) if first else pl.ds((c * chunk - n_q) * nh, chunk * nh)
        return (pltpu.make_async_copy(ck_ref.at[0, src], cko_ref.at[b, dst], sem.at[0]),
                pltpu.make_async_copy(cv_ref.at[0, src], cvo_ref.at[b, dst], sem.at[1]))

    def new_row_copies():
        rows = pl.ds((buf - n_q) * N_ATT_HEADS, n_q * N_ATT_HEADS)
        return (pltpu.make_async_copy(knew_s, cko_ref.at[b, rows], sem.at[2]),
                pltpu.make_async_copy(vnew_s, cvo_ref.at[b, rows], sem.at[3]))

    @pl.when(c == 0)
    def _():
        for cp in shift_copies(True):
            cp.start()
        for h in heads:
            qn_s[h] = _rms(q_ref[:, head_cols[h]], gq_ref[...])
        m_s[...] = jnp.full(m_s.shape, NEG_INF, _F32)
        l_s[...] = jnp.zeros(l_s.shape, _F32)
        acc_s[...] = jnp.zeros(acc_s.shape, _F32)

    @pl.when(c > 0)
    def _():
        for cp in shift_copies(False):
            cp.start()

    def merge(h, logits):
        m_old = m_s[h]
        m_new = functools.reduce(jnp.maximum, [jnp.max(x, axis=-1, keepdims=True) for x in logits],
                                 m_old)
        p = functools.reduce(lambda a, x: a + x, [jnp.exp(x - m_new) for x in logits])
        alpha = jnp.exp(m_old - m_new)
        m_s[h] = m_new
        l_s[h] = alpha * l_s[h] + jnp.sum(p, axis=-1, keepdims=True)
        return p.astype(_BF16), alpha

    qb = [qn_s[h].astype(_BF16) for h in heads]
    head_rows = lambda h, n: pl.ds(h, n, stride=N_ATT_HEADS)
    lc = [_dot_nt(qb[h], ck_ref[0, head_rows(h, chunk), :].astype(_BF16)) * scale for h in heads]
    pa = [merge(h, [lc[h] + bc_ref[h, br] for br in range(n_br)]) for h in heads]
    pv = [_dot(pa[h][0], cv_ref[0, head_rows(h, chunk), :].astype(_BF16)) for h in heads]
    for h in heads:
        acc_s[h] = pa[h][1] * acc_s[h] + pv[h]

    @pl.when(c == last)
    def _():
        for h in heads:
            kn = _rms(k_ref[:, head_cols[h]], gk_ref[...])
            vn = v_ref[:, head_cols[h]]
            ln = _dot_nt(qb[h], kn.astype(_BF16)) * scale
            p, alpha = merge(h, [ln + bn_ref[h, br] for br in range(n_br)])
            acc = alpha * acc_s[h] + _dot(p, vn.astype(_BF16))
            o_ref[:, head_cols[h]] = acc / l_s[h]
            knew_s[head_rows(h, n_q), :] = kn
            vnew_s[head_rows(h, n_q), :] = vn
        for cp in new_row_copies():
            cp.start()
        for cp in new_row_copies():
            cp.wait()

    @pl.when(c == 0)
    def _():
        for cp in shift_copies(True):
            cp.wait()

    @pl.when(c > 0)
    def _():
        for cp in shift_copies(False):
            cp.wait()


def _attn_sample(proj, cache_k, cache_v, bias_c, bias_n, gq, gk, batch, n_q):
    buf = cache_k.shape[1]
    chunk = _SAMPLE_CHUNK
    n_br = len(DILATED_BRANCHES)
    new_spec = lambda g: pl.BlockSpec((n_q, D_ATT), lambda b, c: (b, g))
    flat = (batch, buf * N_ATT_HEADS, HEAD_DIM)
    cache_spec = pl.BlockSpec((1, chunk * N_ATT_HEADS, HEAD_DIM), lambda b, c: (b, c, 0))
    any_spec = pl.BlockSpec(memory_space=pl.ANY)
    per_head = lambda *tail: pltpu.VMEM((N_ATT_HEADS,) + tail, _F32)
    return pl.pallas_call(
        functools.partial(_attn_sample_kernel, n_q=n_q, buf=buf),
        grid=(batch, buf // chunk),
        in_specs=[new_spec(0), new_spec(1), new_spec(2), cache_spec, cache_spec,
                  pl.BlockSpec((N_ATT_HEADS, n_br, n_q, chunk), lambda b, c: (0, 0, 0, c)),
                  pl.BlockSpec((N_ATT_HEADS, n_br, n_q, n_q), lambda b, c: (0, 0, 0, 0)),
                  pl.BlockSpec((1, HEAD_DIM), lambda b, c: (0, 0)),
                  pl.BlockSpec((1, HEAD_DIM), lambda b, c: (0, 0))],
        out_specs=[pl.BlockSpec((n_q, D_ATT), lambda b, c: (b, 0)), any_spec, any_spec],
        out_shape=[jax.ShapeDtypeStruct((batch * n_q, D_ATT), _F32),
                   jax.ShapeDtypeStruct(flat, _F32),
                   jax.ShapeDtypeStruct(flat, _F32)],
        scratch_shapes=[per_head(n_q, HEAD_DIM), per_head(n_q, 1), per_head(n_q, 1),
                        per_head(n_q, HEAD_DIM),
                        pltpu.VMEM((n_q * N_ATT_HEADS, HEAD_DIM), _F32),
                        pltpu.VMEM((n_q * N_ATT_HEADS, HEAD_DIM), _F32),
                        pltpu.SemaphoreType.DMA((4,))],
        compiler_params=_params(("arbitrary", "arbitrary"), 40),
        name="attn_sample",
    )(proj, proj, proj, cache_k.reshape(flat), cache_v.reshape(flat), bias_c, bias_n,
      gq.reshape(1, HEAD_DIM), gk.reshape(1, HEAD_DIM))


def _rglru_kernel(xr_ref, gate_ref, cprev_ref, hprev_ref, cw_ref, cb_ref, wr_ref, br_ref,
                  wi_ref, bi_ref, lam_ref, y_ref, cnew_ref, hnew_ref,
                  xpad_s, a_s, u_s, hs_s, *, n_seq, length):
    pad = V7X_SUBLANES
    z = -lam_ref[...]
    softplus = jnp.maximum(z, 0.0) + jnp.log1p(jnp.exp(-jnp.abs(z)))
    row = lax.broadcasted_iota(jnp.int32, (V7X_SUBLANES, RNN_BLOCK_W), 0)

    for s in range(n_seq):
        rows = slice(s * length, (s + 1) * length)
        xpad_s[pad - (CONV_W - 1):pad, :] = cprev_ref[s]
        xpad_s[pad:, :] = xr_ref[rows, :]
        xc = cb_ref[...] + sum(xpad_s[pad - (CONV_W - 1) + j:pad - (CONV_W - 1) + j + length, :]
                               * cw_ref[j:j + 1, :] for j in range(CONV_W))
        cnew_ref[s] = xpad_s[length + pad - (CONV_W - 1):length + pad, :]
        xcb = xc.astype(_BF16)
        r = jax.nn.sigmoid(_dot(xcb, wr_ref[0]) + br_ref[0])
        i = jax.nn.sigmoid(_dot(xcb, wi_ref[0]) + bi_ref[0])
        log_a = -LRU_C * r * softplus
        a_s[...] = jnp.exp(log_a)
        u_s[...] = jnp.sqrt(1.0 - jnp.exp(2.0 * log_a)) * i * xc

        def group(g, h_prev):
            sl = pl.ds(pl.multiple_of(g * V7X_SUBLANES, V7X_SUBLANES), V7X_SUBLANES)
            a = a_s[sl, :]
            u = u_s[sl, :]
            for sh in (1, 2, 4):
                a_sh = jnp.where(row >= sh, pltpu.roll(a, sh, 0), 1.0)
                u_sh = jnp.where(row >= sh, pltpu.roll(u, sh, 0), 0.0)
                u = a * u_sh + u
                a = a * a_sh
            h = a * h_prev + u
            hs_s[sl, :] = h
            return h[V7X_SUBLANES - 1:, :]

        n_groups = length // V7X_SUBLANES
        h_last = lax.fori_loop(0, n_groups, group, hprev_ref[s], unroll=min(n_groups, 8))
        hnew_ref[s] = h_last
        y_ref[rows, :] = (hs_s[...] * jax.nn.gelu(gate_ref[rows, :])).astype(y_ref.dtype)


def _rglru(proj, conv_prev, h_prev, conv_w, conv_b, w_r, b_r, w_i, b_i, lam, n_seq_total, length,
           seq_per_step):
    t = n_seq_total * length
    rows = seq_per_step * length
    xr_col = 3 * D_ATT // RNN_BLOCK_W
    gate_col = xr_col + RNN_BLOCKS
    vec = lambda a: a.reshape(1, D_RNN)
    vec_spec = pl.BlockSpec((1, RNN_BLOCK_W), lambda s, n: (0, n))
    blk_w = pl.BlockSpec((1, RNN_BLOCK_W, RNN_BLOCK_W), lambda s, n: (n, 0, 0))
    blk_b = pl.BlockSpec((1, 1, RNN_BLOCK_W), lambda s, n: (n, 0, 0))
    return pl.pallas_call(
        functools.partial(_rglru_kernel, n_seq=seq_per_step, length=length),
        grid=(n_seq_total // seq_per_step, RNN_BLOCKS),
        in_specs=[pl.BlockSpec((rows, RNN_BLOCK_W), lambda s, n: (s, xr_col + n)),
                  pl.BlockSpec((rows, RNN_BLOCK_W), lambda s, n: (s, gate_col + n)),
                  pl.BlockSpec((seq_per_step, CONV_W - 1, RNN_BLOCK_W), lambda s, n: (s, 0, n)),
                  pl.BlockSpec((seq_per_step, 1, RNN_BLOCK_W), lambda s, n: (s, 0, n)),
                  pl.BlockSpec((CONV_W, RNN_BLOCK_W), lambda s, n: (0, n)),
                  vec_spec, blk_w, blk_b, blk_w, blk_b, vec_spec],
        out_specs=[pl.BlockSpec((rows, RNN_BLOCK_W), lambda s, n: (s, n)),
                   pl.BlockSpec((seq_per_step, CONV_W - 1, RNN_BLOCK_W), lambda s, n: (s, 0, n)),
                   pl.BlockSpec((seq_per_step, 1, RNN_BLOCK_W), lambda s, n: (s, 0, n))],
        out_shape=[jax.ShapeDtypeStruct((t, D_RNN), _BF16),
                   jax.ShapeDtypeStruct((n_seq_total, CONV_W - 1, D_RNN), _F32),
                   jax.ShapeDtypeStruct((n_seq_total, 1, D_RNN), _F32)],
        scratch_shapes=[pltpu.VMEM((length + V7X_SUBLANES, RNN_BLOCK_W), _F32),
                        pltpu.VMEM((length, RNN_BLOCK_W), _F32),
                        pltpu.VMEM((length, RNN_BLOCK_W), _F32),
                        pltpu.VMEM((length, RNN_BLOCK_W), _F32)],
        compiler_params=_params(("parallel", "arbitrary"), 32),
        name="rglru",
    )(proj, proj, conv_prev, h_prev.reshape(n_seq_total, 1, D_RNN), conv_w, vec(conv_b),
      w_r.astype(_BF16), b_r.reshape(RNN_BLOCKS, 1, RNN_BLOCK_W),
      w_i.astype(_BF16), b_i.reshape(RNN_BLOCKS, 1, RNN_BLOCK_W), vec(lam))


def _extract_max(s):
    m = jnp.max(s, axis=0, keepdims=True)
    return m, jnp.where(s == m, -jnp.inf, s)


def _peer_topk_kernel(q_ref, keys_ref, s1_ref, e1_ref, t2_ref, e2_ref):
    q = q_ref[...]
    half = PEER_QDIM // 2
    s1 = _dot_nt(keys_ref[0, 0], q[:, :half], precision=lax.Precision.HIGHEST)
    s2 = _dot_nt(keys_ref[0, 1], q[:, half:], precision=lax.Precision.HIGHEST)
    tm = s1.shape[1]
    krow = lax.broadcasted_iota(jnp.int32, (PEER_TOPK, tm), 0)

    def top_values(s):
        vals = jnp.zeros((PEER_TOPK, tm), _F32)
        for k in range(PEER_TOPK):
            m, s = _extract_max(s)
            vals = jnp.where(krow == k, m, vals)
        return vals

    v1 = top_values(s1)
    v2 = top_values(s2)
    sub = V7X_SUBLANES
    cand = jnp.concatenate(
        [v1[0:1, :] + v2]
        + [v1[a:a + 1, :] + v2[:sub, :] for a in range(1, sub)]
        + [v1[sub:, :] + v2[0:1, :]], axis=0)
    c_max, cand = _extract_max(cand)
    z = jnp.ones_like(c_max)
    c_k = c_max
    for _ in range(PEER_TOPK - 1):
        c_k, cand = _extract_max(cand)
        z = z + jnp.exp(c_k - c_max)
    c_next, _ = _extract_max(cand)
    theta = 0.5 * (c_k + c_next)
    s1_ref[0] = s1
    e1_ref[0] = jnp.exp(s1 - v1[0:1, :]) / z
    t2_ref[0] = s2 - theta
    e2_ref[0] = jnp.exp(s2 - v2[0:1, :])


def _peer_topk(qp, keys):
    t = qp.shape[0]
    tm = min(t, 512)
    spec = pl.BlockSpec((1, PEER_KEYS, tm), lambda i, h: (h, 0, i))
    shape = jax.ShapeDtypeStruct((PEER_HEADS, PEER_KEYS, t), _F32)
    return pl.pallas_call(
        _peer_topk_kernel,
        grid=(t // tm, PEER_HEADS),
        in_specs=[pl.BlockSpec((tm, PEER_QDIM), lambda i, h: (i, h)),
                  pl.BlockSpec((1, 2, PEER_KEYS, PEER_QDIM // 2), lambda i, h: (h, 0, 0, 0))],
        out_specs=[spec, spec, spec, spec],
        out_shape=[shape, shape, shape, shape],
        compiler_params=_params(("parallel", "arbitrary"), 32),
        name="peer_topk",
    )(qp, keys)


_PEER_EXPERT_TILE = 512


def _peer_experts_kernel(hn_ref, u_ref, v_ref, s1_ref, e1_ref, t2_ref, e2_ref, o_ref, pre_s, gate_s,
                         *, n_tiles):
    j = pl.program_id(1)
    groups = _PEER_EXPERT_TILE // PEER_KEYS

    @pl.when(j == 0)
    def _():
        o_ref[...] = jnp.zeros(o_ref.shape, _F32)
        pre_s[1] = jnp.zeros(pre_s.shape[1:], _F32)
        gate_s[1] = jnp.zeros(gate_s.shape[1:], _F32)

    slot = jnp.bitwise_and(j, 1)
    pre = pre_s[1 - slot]
    act = 0.5 * pre * (1.0 + lax.erf(pre * (2.0 ** -0.5)))
    o_ref[...] += _dot((act * gate_s[1 - slot]).astype(_BF16), v_ref[...])

    pre_s[slot] = _dot_nt(hn_ref[...], u_ref[...])
    tile = jnp.minimum(j, n_tiles - 1)
    for g in range(groups):
        r = jnp.bitwise_and(tile * groups + g, V7X_SUBLANES - 1)
        acc = None
        for h in range(PEER_HEADS):
            s1_row = s1_ref[h, 0, pl.ds(r, 1), :]
            e1_row = e1_ref[h, 0, pl.ds(r, 1), :]
            hit = (s1_row + t2_ref[h]) >= 0.0
            term = jnp.where(hit, e1_row * e2_ref[h], 0.0)
            acc = term if acc is None else acc + term
        gate_s[slot, :, g * PEER_KEYS:(g + 1) * PEER_KEYS] = acc.T


def _peer_experts(hn, u_bf16, v_bf16, s1, e1, t2, e2):
    t = hn.shape[0]
    tm = min(t, 512)
    te = _PEER_EXPERT_TILE
    n_tiles = PEER_EXPERTS // te
    groups = te // PEER_KEYS
    sub = V7X_SUBLANES
    prev = lambda j: jnp.maximum(j - 1, 0)
    cur = lambda j: jnp.minimum(j, n_tiles - 1)
    once = pl.Buffered(1)
    row_view = lambda a: a.reshape(PEER_HEADS, PEER_KEYS // sub, sub, t)
    row_spec = pl.BlockSpec((PEER_HEADS, 1, sub, tm), lambda i, j: (0, (cur(j) * groups) // sub, 0, i))
    full_spec = pl.BlockSpec((PEER_HEADS, PEER_KEYS, tm), lambda i, j: (0, 0, i), pipeline_mode=once)
    return pl.pallas_call(
        functools.partial(_peer_experts_kernel, n_tiles=n_tiles),
        grid=(t // tm, n_tiles + 1),
        in_specs=[pl.BlockSpec((tm, D_MODEL), lambda i, j: (i, 0), pipeline_mode=once),
                  pl.BlockSpec((te, D_MODEL), lambda i, j: (cur(j), 0)),
                  pl.BlockSpec((te, D_MODEL), lambda i, j: (prev(j), 0)),
                  row_spec, row_spec, full_spec, full_spec],
        out_specs=pl.BlockSpec((tm, D_MODEL), lambda i, j: (i, 0)),
        out_shape=jax.ShapeDtypeStruct((t, D_MODEL), _F32),
        scratch_shapes=[pltpu.VMEM((2, tm, te), _F32), pltpu.VMEM((2, tm, te), _F32)],
        compiler_params=_params(("parallel", "arbitrary"), 56),
        name="peer_experts",
    )(hn, u_bf16, v_bf16, row_view(s1), row_view(e1), t2, e2)


def _layer_tail(x2d, attn, y_rnn, w_out, norm2_g, w_q, keys, u_bf16, v_bf16):
    h = _out_proj(attn, y_rnn, w_out, x2d)
    hn = _rmsnorm_bf16(h, norm2_g)
    qp = _matmul(hn, w_q, "peer_query")
    s1, e1, t2, e2 = _peer_topk(qp, keys)
    ffn = _peer_experts(hn, u_bf16, v_bf16, s1, e1, t2, e2)
    return _add(h, ffn)


def kernel(x_prompt, x_sample, cache_win_k, cache_win_v, state_conv, state_rglru, rel_bias, norm1_g,
           w_in, q_norm_g, k_norm_g, conv_w, conv_b, w_rgate, b_rgate, w_igate, b_igate, lru_lambda,
           w_out, norm2_g, peer_w_q, peer_keys, peer_u, peer_v):
    batch, seq, _ = x_prompt.shape
    dec_batch, dec_seq, _ = x_sample.shape
    buf = cache_win_k.shape[2]
    assert norm1_g.shape[0] == 1, "single-layer trunk"
    assert seq % (BRANCH_STEPS * DILATED_BRANCHES[-1][1]) == 0 and seq <= 2048
    assert dec_seq == V7X_SUBLANES and buf >= DILATED_BRANCHES[-1][0] and buf % _SAMPLE_CHUNK == 0

    w_in_b = w_in[0].astype(_BF16)
    w_out_b = w_out[0].astype(_BF16)
    w_q_b = peer_w_q[0].astype(_BF16)
    u_b = peer_u[0].astype(_BF16)
    v_b = peer_v[0].astype(_BF16)
    tail = functools.partial(_layer_tail, w_out=w_out_b, norm2_g=norm2_g[0], w_q=w_q_b,
                             keys=peer_keys[0], u_bf16=u_b, v_bf16=v_b)
    rnn_w = (conv_w[0], conv_b[0], w_rgate[0], b_rgate[0], w_igate[0], b_igate[0], lru_lambda[0])

    xp = x_prompt.reshape(batch * seq, D_MODEL)
    proj_p = _matmul(_rmsnorm_bf16(xp, norm1_g[0]), w_in_b, "in_proj")
    attn_p, k_p, v_p = _attn_prompt(proj_p, _prompt_bias(rel_bias), q_norm_g[0], k_norm_g[0], batch, seq)
    y_rnn_p, conv_p, h_p = _rglru(proj_p, jnp.zeros((batch, CONV_W - 1, D_RNN), _F32),
                                  jnp.zeros((batch, D_RNN), _F32), *rnn_w,
                                  n_seq_total=batch, length=seq, seq_per_step=1)
    y_p = tail(xp, attn_p, y_rnn_p)

    xs = x_sample.reshape(dec_batch * dec_seq, D_MODEL)
    proj_s = _matmul(_rmsnorm_bf16(xs, norm1_g[0]), w_in_b, "in_proj")
    bias_s = _sample_bias(rel_bias, dec_seq, buf)
    attn_s, k_s, v_s = _attn_sample(proj_s, cache_win_k[0], cache_win_v[0],
                                    bias_s[..., :buf], bias_s[..., buf:], q_norm_g[0], k_norm_g[0],
                                    dec_batch, dec_seq)
    y_rnn_s, conv_s, h_s = _rglru(proj_s, state_conv[0], state_rglru[0], *rnn_w,
                                  n_seq_total=dec_batch, length=dec_seq, seq_per_step=dec_batch)
    y_s = tail(xs, attn_s, y_rnn_s)

    heads = (N_ATT_HEADS, HEAD_DIM)
    return (y_p.reshape(batch, seq, D_MODEL), y_s.reshape(dec_batch, dec_seq, D_MODEL),
            k_p.reshape(1, batch, seq, *heads), v_p.reshape(1, batch, seq, *heads),
            conv_p[None], h_p.reshape(1, batch, D_RNN),
            k_s.reshape(1, dec_batch, buf, *heads), v_s.reshape(1, dec_batch, buf, *heads),
            conv_s[None], h_s.reshape(1, dec_batch, D_RNN))
```
